```python
import jax, jax.numpy as jnp
from jax import lax
import numpy as np

D_MODEL = 4096
BATCH = 4
SEQ = 4096
DEPTH = 1

A_WIDTH = D_MODEL // 2
A_GROUPS = 8
A_CHUNK = 128
B_HEAD_DIM = 128
B_HEADS = (D_MODEL // 2) // B_HEAD_DIM
B_WIDTH = B_HEADS * B_HEAD_DIM
B_QBLOCK = 128
N_BRANCHES = 2
IN_COLS = 2 * A_WIDTH + 3 * B_WIDTH + N_BRANCHES * D_MODEL
N_EXPERTS = 64
TOP_K = 6
N_GROUPS = 8
TOPK_GROUPS = 4
EXPERT_FF = D_MODEL // 8
SHARED_FF = D_MODEL // 4
ROUTED_SCALE = 2.5
N_MOD = 6
EPS = 1e-6

kernel_name = "hybrid_sgu_stickbreak_moe_adaln"


def rmsnorm(x, g):
    xf = x.astype(jnp.float32)
    y = xf * lax.rsqrt(jnp.mean(xf * xf, axis=-1, keepdims=True) + EPS)
    return (y * g.astype(jnp.float32)).astype(x.dtype)


def modulate(h, shift, scale):
    return h * (1 + scale[:, None, :]) + shift[:, None, :]


def chunked_spatial_gating(u, v, w_s, b_s):
    bn, s, _ = u.shape
    nc = s // A_CHUNK
    dg = A_WIDTH // A_GROUPS
    u = jax.nn.gelu(u).reshape(bn, nc, A_CHUNK, A_GROUPS, dg)
    vf = jax.nn.gelu(v).astype(jnp.float32).reshape(bn, nc, A_CHUNK, A_GROUPS, dg)
    mu = jnp.mean(vf, axis=-1, keepdims=True)
    var = jnp.mean(jnp.square(vf - mu), axis=-1, keepdims=True)
    vn = ((vf - mu) * lax.rsqrt(var + EPS)).astype(v.dtype)
    causal = jnp.tril(jnp.ones((A_CHUNK, A_CHUNK), dtype=bool))
    ws = jnp.where(causal[None], w_s, jnp.zeros_like(w_s))
    mixed = jnp.einsum('gts,bcsgd->bctgd', ws, vn) + b_s.T[None, None, :, :, None]
    return (u * mixed).reshape(bn, s, A_WIDTH)


def stick_breaking_attention(q, k, v):
    bn, s, h, dh = q.shape
    nb = s // B_QBLOCK
    scale = dh ** -0.5
    key_pos = jnp.arange(s)

    def block(i):
        qb = lax.dynamic_slice_in_dim(q, i * B_QBLOCK, B_QBLOCK, axis=1)
        z = jnp.einsum('bthd,bshd->bhts', qb, k).astype(jnp.float32) * scale
        q_pos = i * B_QBLOCK + jnp.arange(B_QBLOCK)
        causal = key_pos[None, :] < q_pos[:, None]
        log_stay = jnp.where(causal, jax.nn.log_sigmoid(-z), 0.0)
        later = lax.cumsum(log_stay, axis=3, reverse=True) - log_stay
        w = jnp.where(causal, jnp.exp(jax.nn.log_sigmoid(z) + later), 0.0)
        return jnp.einsum('bhts,bshd->bthd', w.astype(v.dtype), v)

    out = lax.map(block, jnp.arange(nb))
    return out.transpose(1, 0, 2, 3, 4).reshape(bn, s, h * dh)


def mixer_sublayer(h, w_in, w_s, b_s, w_up_a, w_up_b, w_o):
    bn, s, _ = h.shape
    proj = h @ w_in
    cuts = [int(c) for c in np.cumsum([A_WIDTH, A_WIDTH, B_WIDTH, B_WIDTH, B_WIDTH, D_MODEL])]
    u, v, q, k, vb, ga, gb = jnp.split(proj, cuts, axis=-1)
    ya = chunked_spatial_gating(u, v, w_s, b_s) @ w_up_a
    hs = (bn, s, B_HEADS, B_HEAD_DIM)
    yb = stick_breaking_attention(q.reshape(hs), k.reshape(hs), vb.reshape(hs)) @ w_up_b
    merged = jax.nn.sigmoid(ga) * ya + jax.nn.sigmoid(gb) * yb
    return merged @ w_o


def swiglu(t, wg, wu, wd):
    return (jax.nn.silu(t @ wg) * (t @ wu)) @ wd


def moe_sublayer(h, w_router, router_bias, w_gate_e, w_up_e, w_down_e, w_gate_s, w_up_s, w_down_s):
    bn, s, d = h.shape
    t = h.reshape(-1, d)
    scores = jax.nn.sigmoid((t @ w_router).astype(jnp.float32))
    choice = scores + router_bias.astype(jnp.float32)
    per_group = N_EXPERTS // N_GROUPS
    group_score = lax.top_k(choice.reshape(-1, N_GROUPS, per_group), 2)[0].sum(-1)
    _, top_groups = lax.top_k(group_score, TOPK_GROUPS)
    group_mask = jnp.sum(jax.nn.one_hot(top_groups, N_GROUPS, dtype=jnp.float32), axis=1) > 0
    expert_mask = jnp.repeat(group_mask, per_group, axis=1)
    _, idx = lax.top_k(jnp.where(expert_mask, choice, -jnp.inf), TOP_K)
    w = jnp.take_along_axis(scores, idx, axis=1)
    w = w / jnp.sum(w, axis=-1, keepdims=True) * ROUTED_SCALE
    combine = jnp.einsum('nk,nke->ne', w, jax.nn.one_hot(idx, N_EXPERTS, dtype=jnp.float32))

    def expert_step(acc, params):
        wg, wu, wd, cw = params
        return acc + cw[:, None].astype(t.dtype) * swiglu(t, wg, wu, wd), None

    routed, _ = lax.scan(expert_step, jnp.zeros_like(t), (w_gate_e, w_up_e, w_down_e, combine.T))
    return (routed + swiglu(t, w_gate_s, w_up_s, w_down_s)).reshape(bn, s, d)


def _normal(key, shape, scale):
    return jax.random.normal(key, shape, jnp.float32) * scale


def setup_inputs(seed: int = 0) -> dict:
    key = jax.random.key(seed)
    ks = jax.random.split(key, 21)
    L, D, E = DEPTH, D_MODEL, N_EXPERTS
    return {
        "x": _normal(ks[0], (BATCH, SEQ, D), 1.0),
        "c": _normal(ks[1], (BATCH, D), 1.0),
        "w_ada": _normal(ks[2], (L, D, N_MOD * D), 0.5 * D ** -0.5),
        "b_ada": _normal(ks[3], (L, N_MOD * D), 0.02),
        "norm1_g": 1.0 + _normal(ks[4], (L, D), 0.02),
        "w_in": _normal(ks[5], (L, D, IN_COLS), D ** -0.5),
        "w_s": _normal(ks[6], (L, A_GROUPS, A_CHUNK, A_CHUNK), A_CHUNK ** -0.5),
        "b_s": 1.0 + _normal(ks[7], (L, A_GROUPS, A_CHUNK), 0.02),
        "w_up_a": _normal(ks[8], (L, A_WIDTH, D), A_WIDTH ** -0.5),
        "w_up_b": _normal(ks[9], (L, B_WIDTH, D), B_WIDTH ** -0.5),
        "w_o": _normal(ks[10], (L, D, D), D ** -0.5),
        "norm2_g": 1.0 + _normal(ks[11], (L, D), 0.02),
        "w_router": _normal(ks[12], (L, D, E), D ** -0.5),
        "router_bias": _normal(ks[13], (L, E), 0.01),
        "w_gate_e": _normal(ks[14], (L, E, D, EXPERT_FF), D ** -0.5),
        "w_up_e": _normal(ks[15], (L, E, D, EXPERT_FF), D ** -0.5),
        "w_down_e": _normal(ks[16], (L, E, EXPERT_FF, D), EXPERT_FF ** -0.5),
        "w_gate_s": _normal(ks[17], (L, D, SHARED_FF), D ** -0.5),
        "w_up_s": _normal(ks[18], (L, D, SHARED_FF), D ** -0.5),
        "w_down_s": _normal(ks[19], (L, SHARED_FF, D), SHARED_FF ** -0.5),
        "final_g": 1.0 + _normal(ks[20], (D,), 0.02),
    }


def reference(x, c, w_ada, b_ada, norm1_g, w_in, w_s, b_s, w_up_a, w_up_b, w_o, norm2_g,
              w_router, router_bias, w_gate_e, w_up_e, w_down_e, w_gate_s, w_up_s, w_down_s, final_g):
    c_act = jax.nn.silu(c)
    for l in range(DEPTH):
        mod = c_act @ w_ada[l] + b_ada[l]
        shift1, scale1, gate1, shift2, scale2, gate2 = jnp.split(mod, N_MOD, axis=-1)
        h = modulate(rmsnorm(x, norm1_g[l]), shift1, scale1)
        x = x + gate1[:, None, :] * mixer_sublayer(h, w_in[l], w_s[l], b_s[l], w_up_a[l], w_up_b[l], w_o[l])
        h = modulate(rmsnorm(x, norm2_g[l]), shift2, scale2)
        x = x + gate2[:, None, :] * moe_sublayer(h, w_router[l], router_bias[l], w_gate_e[l], w_up_e[l],
                                                 w_down_e[l], w_gate_s[l], w_up_s[l], w_down_s[l])
    return rmsnorm(x, final_g)
```

```python
import functools

import jax
import jax.numpy as jnp
from jax import lax
from jax.experimental import pallas as pl
from jax.experimental.pallas import tpu as pltpu

F32 = jnp.float32
BF16 = jnp.bfloat16
I32 = jnp.int32
U32 = jnp.uint32

EPS = 1e-6
A_GROUPS = 8
A_CHUNK = 128
HEAD_DIM = 128
N_EXPERTS = 64
TOP_K = 6
N_GROUPS = 8
TOPK_GROUPS = 4
ROUTED_SCALE = 2.5
N_MOD = 6
K_PAD = 8

VMEM_LIMIT = 56 * 1024 * 1024


def _cparams(sem):
    return pltpu.CompilerParams(dimension_semantics=sem, vmem_limit_bytes=VMEM_LIMIT)


def _dot(a, b):
    return jnp.dot(a, b, preferred_element_type=F32)


def _dot_nt(a, b):
    return lax.dot_general(a, b, (((1,), (1,)), ((), ())), preferred_element_type=F32)


def _pack_halves(h):
    c = h.shape[1] // 2
    lo = pltpu.bitcast(h[:, :c].astype(BF16).astype(F32), U32)
    hi = pltpu.bitcast(h[:, c:].astype(BF16).astype(F32), U32)
    return (hi & jnp.uint32(0xFFFF0000)) | (lo >> 16)


def _unpack_halves(p):
    lo = pltpu.bitcast(p << 16, F32)
    hi = pltpu.bitcast(p & jnp.uint32(0xFFFF0000), F32)
    return lo, hi


def _ada_kernel(c_ref, w_ref, b_ref, o_ref):
    c = c_ref[...]
    cact = (c * jax.nn.sigmoid(c)).astype(BF16)
    o_ref[...] = _dot(cact, w_ref[...].astype(BF16)) + b_ref[...]


def _ada(c, w_ada, b_ada, bn=512):
    b, d = c.shape
    n = w_ada.shape[1]
    cp = jnp.zeros((8, d), F32).at[:b].set(c)
    out = pl.pallas_call(
        _ada_kernel,
        grid=(n // bn,),
        in_specs=[
            pl.BlockSpec((8, d), lambda j: (0, 0)),
            pl.BlockSpec((d, bn), lambda j: (0, j)),
            pl.BlockSpec((1, bn), lambda j: (0, j)),
        ],
        out_specs=pl.BlockSpec((8, bn), lambda j: (0, j)),
        out_shape=jax.ShapeDtypeStruct((8, n), F32),
        compiler_params=_cparams(("arbitrary",)),
        name="ada",
    )(cp, w_ada, b_ada.reshape(1, n))
    return out[:b].reshape(b, N_MOD, d)


def _norm_mod(x, g_ref, mod_ref, shift_idx, scale_idx):
    y = x * lax.rsqrt(jnp.mean(x * x, axis=-1, keepdims=True) + EPS) * g_ref[...]
    mod = mod_ref[0]
    return y * (1.0 + mod[scale_idx:scale_idx + 1]) + mod[shift_idx:shift_idx + 1]


def _norm1_kernel(x_ref, g_ref, mod_ref, o_ref):
    o_ref[...] = _norm_mod(x_ref[...], g_ref, mod_ref, 0, 1).astype(BF16)


def _norm1(x2, g, mod, seq, bm=512):
    n, d = x2.shape
    return pl.pallas_call(
        _norm1_kernel,
        grid=(n // bm,),
        in_specs=[
            pl.BlockSpec((bm, d), lambda i: (i, 0)),
            pl.BlockSpec((1, d), lambda i: (0, 0)),
            pl.BlockSpec((1, N_MOD, d), lambda i: (i * bm // seq, 0, 0)),
        ],
        out_specs=pl.BlockSpec((bm, d), lambda i: (i, 0)),
        out_shape=jax.ShapeDtypeStruct((n, d), BF16),
        compiler_params=_cparams(("arbitrary",)),
        name="norm1",
    )(x2, g.reshape(1, d), mod)


def _inproj_kernel(h_ref, w_ref, o_ref, *, n_gelu, n_ln, n_plain, group):
    j = pl.program_id(1)
    acc = _dot(h_ref[...], w_ref[...].astype(BF16))

    @pl.when(j < n_gelu)
    def _():
        o_ref[...] = jax.nn.gelu(acc).astype(o_ref.dtype)

    @pl.when((j >= n_gelu) & (j < n_ln))
    def _():
        a = jax.nn.gelu(acc)
        for s in range(acc.shape[1] // group):
            seg = a[:, s * group:(s + 1) * group]
            mu = jnp.mean(seg, axis=-1, keepdims=True)
            cen = seg - mu
            var = jnp.mean(cen * cen, axis=-1, keepdims=True)
            o_ref[:, s * group:(s + 1) * group] = (cen * lax.rsqrt(var + EPS)).astype(o_ref.dtype)

    @pl.when((j >= n_ln) & (j < n_plain))
    def _():
        o_ref[...] = acc.astype(o_ref.dtype)

    @pl.when(j >= n_plain)
    def _():
        o_ref[...] = jax.nn.sigmoid(acc).astype(o_ref.dtype)


def _inproj(h, w_in, a_width, b_width, bm=1024, bn=512):
    n, d = h.shape
    cols = w_in.shape[1]
    group = a_width // A_GROUPS
    kern = functools.partial(
        _inproj_kernel,
        n_gelu=a_width // bn,
        n_ln=2 * a_width // bn,
        n_plain=(2 * a_width + 3 * b_width) // bn,
        group=group,
    )
    return pl.pallas_call(
        kern,
        grid=(n // bm, cols // bn),
        in_specs=[
            pl.BlockSpec((bm, d), lambda i, j: (i, 0)),
            pl.BlockSpec((d, bn), lambda i, j: (0, j)),
        ],
        out_specs=pl.BlockSpec((bm, bn), lambda i, j: (i, j)),
        out_shape=jax.ShapeDtypeStruct((n, cols), BF16),
        compiler_params=_cparams(("arbitrary", "arbitrary")),
        name="inproj",
    )(h, w_in)


def _sgu_kernel(u_ref, v_ref, ws_ref, bs_ref, o_ref, *, group):
    t = A_CHUNK
    row = lax.broadcasted_iota(I32, (t, t), 0)
    col = lax.broadcasted_iota(I32, (t, t), 1)
    for g in range(A_GROUPS):
        ws = jnp.where(row >= col, ws_ref[g], 0.0).astype(BF16)
        bias = bs_ref[:, g:g + 1]
        for c in range(u_ref.shape[0] // t):
            rs = slice(c * t, (c + 1) * t)
            cs = slice(g * group, (g + 1) * group)
            mixed = _dot(ws, v_ref[rs, cs]) + bias
            o_ref[rs, cs] = (u_ref[rs, cs].astype(F32) * mixed).astype(o_ref.dtype)


def _sgu(proj, w_s, b_s, a_width, bm=512):
    n = proj.shape[0]
    group = a_width // A_GROUPS
    return pl.pallas_call(
        functools.partial(_sgu_kernel, group=group),
        grid=(n // bm,),
        in_specs=[
            pl.BlockSpec((bm, a_width), lambda i: (i, 0)),
            pl.BlockSpec((bm, a_width), lambda i: (i, 1)),
            pl.BlockSpec((A_GROUPS, A_CHUNK, A_CHUNK), lambda i: (0, 0, 0)),
            pl.BlockSpec((A_CHUNK, A_GROUPS), lambda i: (0, 0)),
        ],
        out_specs=pl.BlockSpec((bm, a_width), lambda i: (i, 0)),
        out_shape=jax.ShapeDtypeStruct((n, a_width), BF16),
        compiler_params=_cparams(("arbitrary",)),
        name="sgu",
    )(proj, proj, w_s, b_s.T)


def _attn_kernel(q_ref, k_ref, v_ref, o_ref, *, blk, scale):
    qi = pl.program_id(2)
    q = q_ref[...]
    row = lax.broadcasted_iota(I32, (blk, blk), 0)
    col = lax.broadcasted_iota(I32, (blk, blk), 1)
    upper = jnp.where(row > col, 1.0, 0.0).astype(BF16)

    def body(n, carry):
        c, acc = carry
        kb = qi - n
        ks = pl.multiple_of(kb * blk, blk)
        k = k_ref[pl.ds(ks, blk), :]
        v = v_ref[pl.ds(ks, blk), :]
        z = _dot_nt(q, k) * scale
        causal = (kb * blk + col) < (qi * blk + row)
        sp = jnp.maximum(z, 0.0) + jnp.log1p(jnp.exp(-jnp.abs(z)))
        log_stay = jnp.where(causal, -sp, 0.0)
        hi = log_stay.astype(BF16)
        lo = (log_stay - hi.astype(F32)).astype(BF16)
        later = _dot(hi, upper) + _dot(lo, upper)
        w = jnp.where(causal, jnp.exp((z - sp) + later + c), 0.0)
        acc = acc + _dot(w.astype(BF16), v)
        c = c + later[:, :1] + log_stay[:, :1]
        return c, acc

    c0 = jnp.zeros((blk, 1), F32)
    acc0 = jnp.zeros((blk, HEAD_DIM), F32)
    _, acc = lax.fori_loop(0, qi + 1, body, (c0, acc0))
    o_ref[...] = acc.astype(o_ref.dtype)


def _attention(proj, batch, seq, q_col, k_col, v_col, n_heads, blk=256):
    nq = seq // blk
    qc, kc, vc = q_col // HEAD_DIM, k_col // HEAD_DIM, v_col // HEAD_DIM
    return pl.pallas_call(
        functools.partial(_attn_kernel, blk=blk, scale=HEAD_DIM ** -0.5),
        grid=(batch, n_heads, nq),
        in_specs=[
            pl.BlockSpec((blk, HEAD_DIM), lambda b, h, i: (b * nq + i, qc + h)),
            pl.BlockSpec((seq, HEAD_DIM), lambda b, h, i: (b, kc + h)),
            pl.BlockSpec((seq, HEAD_DIM), lambda b, h, i: (b, vc + h)),
        ],
        out_specs=pl.BlockSpec((blk, HEAD_DIM), lambda b, h, i: (b * nq + i, h)),
        out_shape=jax.ShapeDtypeStruct((batch * seq, n_heads * HEAD_DIM), BF16),
        compiler_params=_cparams(("arbitrary", "arbitrary", "arbitrary")),
        name="attn",
    )(proj, proj, proj)


def _merge_kernel(a_ref, b_ref, wa_ref, wb_ref, ga_ref, gb_ref, o_ref):
    ya = _dot(a_ref[...], wa_ref[...].astype(BF16))
    yb = _dot(b_ref[...], wb_ref[...].astype(BF16))
    o_ref[...] = (ga_ref[...].astype(F32) * ya + gb_ref[...].astype(F32) * yb).astype(o_ref.dtype)


def _merge(sgu, att, w_up_a, w_up_b, proj, ga_col, gb_col, bm=1024, bn=512):
    n, ka = sgu.shape
    kb = att.shape[1]
    d = w_up_a.shape[1]
    gac, gbc = ga_col // bn, gb_col // bn
    return pl.pallas_call(
        _merge_kernel,
        grid=(n // bm, d // bn),
        in_specs=[
            pl.BlockSpec((bm, ka), lambda i, j: (i, 0)),
            pl.BlockSpec((bm, kb), lambda i, j: (i, 0)),
            pl.BlockSpec((ka, bn), lambda i, j: (0, j)),
            pl.BlockSpec((kb, bn), lambda i, j: (0, j)),
            pl.BlockSpec((bm, bn), lambda i, j: (i, gac + j)),
            pl.BlockSpec((bm, bn), lambda i, j: (i, gbc + j)),
        ],
        out_specs=pl.BlockSpec((bm, bn), lambda i, j: (i, j)),
        out_shape=jax.ShapeDtypeStruct((n, d), BF16),
        compiler_params=_cparams(("arbitrary", "arbitrary")),
        name="merge",
    )(sgu, att, w_up_a, w_up_b, proj, proj)


def _resid_kernel(a_ref, w_ref, x_ref, mod_ref, o_ref, *, gate_idx):
    y = _dot(a_ref[...], w_ref[...].astype(BF16))
    gate = mod_ref[0][gate_idx:gate_idx + 1]
    o_ref[...] = x_ref[...] + gate * y


def _resid_matmul(a, w, x2, mod, gate_idx, seq, name, bm=1024, bn=512):
    n, k = a.shape
    d = w.shape[1]
    return pl.pallas_call(
        functools.partial(_resid_kernel, gate_idx=gate_idx),
        grid=(n // bm, d // bn),
        in_specs=[
            pl.BlockSpec((bm, k), lambda i, j: (i, 0)),
            pl.BlockSpec((k, bn), lambda i, j: (0, j)),
            pl.BlockSpec((bm, bn), lambda i, j: (i, j)),
            pl.BlockSpec((1, N_MOD, bn), lambda i, j: (i * bm // seq, 0, j)),
        ],
        out_specs=pl.BlockSpec((bm, bn), lambda i, j: (i, j)),
        out_shape=jax.ShapeDtypeStruct((n, d), F32),
        compiler_params=_cparams(("arbitrary", "arbitrary")),
        name=name,
    )(a, w, x2, mod)


def _first_argmax(vals, iota, size):
    m = jnp.max(vals, axis=0, keepdims=True)
    idx = jnp.min(jnp.where(vals == m, iota, size), axis=0, keepdims=True)
    return m, idx


def _router_kernel(x_ref, g_ref, mod_ref, wr_ref, rb_ref, h_ref, hp_ref, ids_ref, wts_ref, rank_ref,
                   cnt_ref, carry_ref):
    i = pl.program_id(0)
    bm = x_ref.shape[0]
    e = N_EXPERTS
    per = e // N_GROUPS

    @pl.when(i == 0)
    def _():
        carry_ref[...] = jnp.zeros_like(carry_ref)

    h = _norm_mod(x_ref[...], g_ref, mod_ref, 3, 4)
    h_ref[...] = h.astype(BF16)
    hp_ref[...] = _pack_halves(h)

    h_hi = h.astype(BF16)
    h_lo = (h - h_hi.astype(F32)).astype(BF16)
    wr = wr_ref[...]
    w_hi = wr.astype(BF16)
    w_lo = (wr - w_hi.astype(F32)).astype(BF16)
    logits = _dot_nt(w_hi, h_hi) + (_dot_nt(w_hi, h_lo) + _dot_nt(w_lo, h_hi))
    scores = jax.nn.sigmoid(logits)
    choice = scores + rb_ref[...]

    iota_p = lax.broadcasted_iota(I32, (per, bm), 0)
    gs_rows = []
    for g in range(N_GROUPS):
        cg = choice[g * per:(g + 1) * per, :]
        m1, i1 = _first_argmax(cg, iota_p, per)
        m2 = jnp.max(jnp.where(iota_p == i1, -jnp.inf, cg), axis=0, keepdims=True)
        gs_rows.append(m1 + m2)
    gs = jnp.concatenate(gs_rows, axis=0)

    iota_g = lax.broadcasted_iota(I32, (N_GROUPS, bm), 0)
    gsel = jnp.zeros((N_GROUPS, bm), F32)
    for _ in range(TOPK_GROUPS):
        _, ig = _first_argmax(gs, iota_g, N_GROUPS)
        hit = iota_g == ig
        gsel = jnp.where(hit, 1.0, gsel)
        gs = jnp.where(hit, -jnp.inf, gs)
    emask = jnp.concatenate(
        [jnp.broadcast_to(gsel[g:g + 1, :], (per, bm)) for g in range(N_GROUPS)], axis=0)

    iota_e = lax.broadcasted_iota(I32, (e, bm), 0)
    masked = jnp.where(emask > 0.5, choice, -jnp.inf)
    self_f = jnp.zeros((e, bm), F32)
    idx_rows, w_rows = [], []
    for _ in range(TOP_K):
        _, ie = _first_argmax(masked, iota_e, e)
        hit = iota_e == ie
        self_f = jnp.where(hit, 1.0, self_f)
        idx_rows.append(ie)
        w_rows.append(jnp.sum(jnp.where(hit, scores, 0.0), axis=0, keepdims=True))
        masked = jnp.where(hit, -jnp.inf, masked)
    w = jnp.concatenate(w_rows, axis=0)
    w = w / jnp.sum(w, axis=0, keepdims=True) * ROUTED_SCALE

    t_row = lax.broadcasted_iota(I32, (bm, bm), 0)
    t_col = lax.broadcasted_iota(I32, (bm, bm), 1)
    before = jnp.where(t_row < t_col, 1.0, 0.0).astype(BF16)
    rank_mat = _dot(self_f.astype(BF16), before) + carry_ref[:, :1]
    rank_rows = [jnp.sum(jnp.where(iota_e == ie, rank_mat, 0.0), axis=0, keepdims=True)
                 for ie in idx_rows]
    carry_ref[...] = carry_ref[...] + jnp.sum(self_f, axis=1, keepdims=True)

    pad_i = jnp.zeros((K_PAD - TOP_K, bm), I32)
    ids_ref[...] = jnp.concatenate(idx_rows + [pad_i], axis=0)
    wts_ref[...] = jnp.concatenate([w, jnp.zeros((K_PAD - TOP_K, bm), F32)], axis=0)
    rank_ref[...] = jnp.concatenate([r.astype(I32) for r in rank_rows] + [pad_i], axis=0)
    cnt_ref[...] = carry_ref[...].astype(I32)


def _router(x1, g, mod, w_router, router_bias, seq, bm=512):
    n, d = x1.shape
    e = N_EXPERTS
    tok = lambda i: (0, i)
    return pl.pallas_call(
        _router_kernel,
        grid=(n // bm,),
        in_specs=[
            pl.BlockSpec((bm, d), lambda i: (i, 0)),
            pl.BlockSpec((1, d), lambda i: (0, 0)),
            pl.BlockSpec((1, N_MOD, d), lambda i: (i * bm // seq, 0, 0)),
            pl.BlockSpec((e, d), lambda i: (0, 0)),
            pl.BlockSpec((e, 1), lambda i: (0, 0)),
        ],
        out_specs=[
            pl.BlockSpec((bm, d), lambda i: (i, 0)),
            pl.BlockSpec((bm, d // 2), lambda i: (i, 0)),
            pl.BlockSpec((K_PAD, bm), tok),
            pl.BlockSpec((K_PAD, bm), tok),
            pl.BlockSpec((K_PAD, bm), tok),
            pl.BlockSpec((e, 128), lambda i: (0, 0)),
        ],
        out_shape=[
            jax.ShapeDtypeStruct((n, d), BF16),
            jax.ShapeDtypeStruct((n, d // 2), U32),
            jax.ShapeDtypeStruct((K_PAD, n), I32),
            jax.ShapeDtypeStruct((K_PAD, n), F32),
            jax.ShapeDtypeStruct((K_PAD, n), I32),
            jax.ShapeDtypeStruct((e, 128), I32),
        ],
        scratch_shapes=[pltpu.VMEM((e, 128), F32)],
        compiler_params=_cparams(("arbitrary",)),
        name="router",
    )(x1, g.reshape(1, d), mod, w_router.T, router_bias.reshape(e, 1))


def _pos_kernel(off_ref, ids_ref, rank_ref, o_ref):
    ids = ids_ref[...]
    pos = rank_ref[...]
    for ex in range(N_EXPERTS):
        pos = pos + jnp.where(ids == ex, off_ref[ex], 0)
    o_ref[...] = pos


def _positions(offsets, ids, ranks):
    return pl.pallas_call(
        _pos_kernel,
        in_specs=[
            pl.BlockSpec(memory_space=pltpu.SMEM),
            pl.BlockSpec(memory_space=pltpu.VMEM),
            pl.BlockSpec(memory_space=pltpu.VMEM),
        ],
        out_specs=pl.BlockSpec(memory_space=pltpu.VMEM),
        out_shape=jax.ShapeDtypeStruct(ids.shape, I32),
        name="positions",
    )(offsets, ids, ranks)


def _dispatch_kernel(pos_ref, h_ref, xs_in_ref, xs_ref, sem):
    del xs_in_ref
    bt = h_ref.shape[0]

    def copy(r, k):
        p = pos_ref[0, k, r]
        return pltpu.make_async_copy(h_ref.at[pl.ds(r, 1)], xs_ref.at[pl.ds(p, 1)], sem)

    def issue(r, _):
        for k in range(TOP_K):
            copy(r, k).start()
        return 0

    def drain(r, _):
        for k in range(TOP_K):
            copy(r, k).wait()
        return 0

    lax.fori_loop(0, bt, issue, 0)
    lax.fori_loop(0, bt, drain, 0)


def _dispatch(hp, pos3, xs_init, bt):
    n, c = hp.shape
    return pl.pallas_call(
        _dispatch_kernel,
        grid=(n // bt,),
        in_specs=[
            pl.BlockSpec((1, K_PAD, bt), lambda i: (i, 0, 0), memory_space=pltpu.SMEM),
            pl.BlockSpec((bt, c), lambda i: (i, 0)),
            pl.BlockSpec(memory_space=pl.ANY),
        ],
        out_specs=pl.BlockSpec(memory_space=pl.ANY),
        out_shape=jax.ShapeDtypeStruct(xs_init.shape, U32),
        scratch_shapes=[pltpu.SemaphoreType.DMA(())],
        input_output_aliases={2: 0},
        compiler_params=_cparams(("arbitrary",)),
        name="dispatch",
    )(pos3, hp, xs_init)


def _new_expert(te_ref, t):
    prev = te_ref[jnp.maximum(t - 1, 0)]
    return (t == 0) | (te_ref[t] != prev)


def _expert_up_kernel(te_ref, nu_ref, xs_ref, wg_ref, wu_ref, o_ref, w_scr):
    t = pl.program_id(0)
    f = wg_ref.shape[2]

    @pl.when(_new_expert(te_ref, t))
    def _():
        w_scr[:, :f] = wg_ref[0].astype(BF16)
        w_scr[:, f:] = wu_ref[0].astype(BF16)

    @pl.when(t < nu_ref[0])
    def _():
        lo, hi = _unpack_halves(xs_ref[...])
        x = jnp.concatenate([lo.astype(BF16), hi.astype(BF16)], axis=1)
        gu = _dot(x, w_scr[...])
        g, u = gu[:, :f], gu[:, f:]
        o_ref[...] = (g * jax.nn.sigmoid(g) * u).astype(o_ref.dtype)

    @pl.when(t >= nu_ref[0])
    def _():
        o_ref[...] = jnp.zeros_like(o_ref)


def _expert_up(tile_expert, n_used, xs, w_gate_e, w_up_e, tm):
    p, c = xs.shape
    e, d, f = w_gate_e.shape
    grid_spec = pltpu.PrefetchScalarGridSpec(
        num_scalar_prefetch=2,
        grid=(p // tm,),
        in_specs=[
            pl.BlockSpec((tm, c), lambda t, te, nu: (t, 0)),
            pl.BlockSpec((1, d, f), lambda t, te, nu: (te[t], 0, 0)),
            pl.BlockSpec((1, d, f), lambda t, te, nu: (te[t], 0, 0)),
        ],
        out_specs=pl.BlockSpec((tm, f), lambda t, te, nu: (t, 0)),
        scratch_shapes=[pltpu.VMEM((d, 2 * f), BF16)],
    )
    return pl.pallas_call(
        _expert_up_kernel,
        grid_spec=grid_spec,
        out_shape=jax.ShapeDtypeStruct((p, f), BF16),
        compiler_params=_cparams(("arbitrary",)),
        name="expert_up",
    )(tile_expert, n_used, xs, w_gate_e, w_up_e)


def _expert_down_kernel(te_ref, nu_ref, a_ref, wd_ref, o_ref, w_scr):
    t = pl.program_id(0)

    @pl.when(_new_expert(te_ref, t))
    def _():
        w_scr[...] = wd_ref[0].astype(BF16)

    @pl.when(t < nu_ref[0])
    def _():
        o_ref[...] = _pack_halves(_dot(a_ref[...], w_scr[...]))

    @pl.when(t >= nu_ref[0])
    def _():
        o_ref[...] = jnp.zeros_like(o_ref)


def _expert_down(tile_expert, n_used, act, w_down_e, tm):
    p, f = act.shape
    e, _, d = w_down_e.shape
    grid_spec = pltpu.PrefetchScalarGridSpec(
        num_scalar_prefetch=2,
        grid=(p // tm,),
        in_specs=[
            pl.BlockSpec((tm, f), lambda t, te, nu: (t, 0)),
            pl.BlockSpec((1, f, d), lambda t, te, nu: (te[t], 0, 0)),
        ],
        out_specs=pl.BlockSpec((tm, d // 2), lambda t, te, nu: (t, 0)),
        scratch_shapes=[pltpu.VMEM((f, d), BF16)],
    )
    return pl.pallas_call(
        _expert_down_kernel,
        grid_spec=grid_spec,
        out_shape=jax.ShapeDtypeStruct((p, d // 2), U32),
        compiler_params=_cparams(("arbitrary",)),
        name="expert_down",
    )(tile_expert, n_used, act, w_down_e)


def _shared_up_kernel(h_ref, wg_ref, wu_ref, o_ref):
    h = h_ref[...]
    g = _dot(h, wg_ref[...].astype(BF16))
    u = _dot(h, wu_ref[...].astype(BF16))
    o_ref[...] = (g * jax.nn.sigmoid(g) * u).astype(o_ref.dtype)


def _shared_up(h, w_gate_s, w_up_s, bm=1024, bn=256):
    n, d = h.shape
    f = w_gate_s.shape[1]
    return pl.pallas_call(
        _shared_up_kernel,
        grid=(n // bm, f // bn),
        in_specs=[
            pl.BlockSpec((bm, d), lambda i, j: (i, 0)),
            pl.BlockSpec((d, bn), lambda i, j: (0, j)),
            pl.BlockSpec((d, bn), lambda i, j: (0, j)),
        ],
        out_specs=pl.BlockSpec((bm, bn), lambda i, j: (i, j)),
        out_shape=jax.ShapeDtypeStruct((n, f), BF16),
        compiler_params=_cparams(("arbitrary", "arbitrary")),
        name="shared_up",
    )(h, w_gate_s, w_up_s)


def _combine_kernel(pos_ref, x_ref, wt_ref, mod_ref, g_ref, ys_ref, o_ref, ybuf, sem):
    bt = x_ref.shape[0]

    def copy(r, k):
        p = pos_ref[0, k, r]
        return pltpu.make_async_copy(ys_ref.at[pl.ds(p, 1)], ybuf.at[k, pl.ds(r, 1)], sem)

    def issue(r, _):
        for k in range(TOP_K):
            copy(r, k).start()
        return 0

    def drain(r, _):
        for k in range(TOP_K):
            copy(r, k).wait()
        return 0

    lax.fori_loop(0, bt, issue, 0)
    lax.fori_loop(0, bt, drain, 0)

    c = ybuf.shape[2]
    acc_lo = jnp.zeros((bt, c), F32)
    acc_hi = jnp.zeros((bt, c), F32)
    for k in range(TOP_K):
        lo, hi = _unpack_halves(ybuf[k])
        wk = wt_ref[:, k:k + 1]
        acc_lo = acc_lo + wk * lo
        acc_hi = acc_hi + wk * hi
    routed = jnp.concatenate([acc_lo, acc_hi], axis=1)
    gate = mod_ref[0][5:6]
    x = x_ref[...] + gate * routed
    y = x * lax.rsqrt(jnp.mean(x * x, axis=-1, keepdims=True) + EPS) * g_ref[...]
    o_ref[...] = y


def _combine(pos3, x1s, wts_t, mod, final_g, ys, seq, bt):
    n, d = x1s.shape
    return pl.pallas_call(
        _combine_kernel,
        grid=(n // bt,),
        in_specs=[
            pl.BlockSpec((1, K_PAD, bt), lambda i: (i, 0, 0), memory_space=pltpu.SMEM),
            pl.BlockSpec((bt, d), lambda i: (i, 0)),
            pl.BlockSpec((bt, K_PAD), lambda i: (i, 0)),
            pl.BlockSpec((1, N_MOD, d), lambda i: (i * bt // seq, 0, 0)),
            pl.BlockSpec((1, d), lambda i: (0, 0)),
            pl.BlockSpec(memory_space=pl.ANY),
        ],
        out_specs=pl.BlockSpec((bt, d), lambda i: (i, 0)),
        out_shape=jax.ShapeDtypeStruct((n, d), F32),
        scratch_shapes=[pltpu.VMEM((TOP_K, bt, d // 2), U32), pltpu.SemaphoreType.DMA(())],
        compiler_params=_cparams(("arbitrary",)),
        name="combine",
    )(pos3, x1s, wts_t, mod, final_g.reshape(1, d), ys)


def _moe_tiles(counts, n_tokens, tm):
    e = counts.shape[0]
    n_tiles = n_tokens * TOP_K // tm + e
    tiles_per = (counts + tm - 1) // tm
    tile_end = jnp.cumsum(tiles_per)
    offsets = (tile_end - tiles_per) * tm
    n_used = tile_end[-1]
    t = jnp.arange(n_tiles, dtype=I32)
    te = jnp.searchsorted(tile_end, jnp.minimum(t, n_used - 1), side="right").astype(I32)
    return offsets.astype(I32), te, n_used.reshape(1).astype(I32), n_tiles


def kernel(x, c, w_ada, b_ada, norm1_g, w_in, w_s, b_s, w_up_a, w_up_b, w_o, norm2_g, w_router,
           router_bias, w_gate_e, w_up_e, w_down_e, w_gate_s, w_up_s, w_down_s, final_g):
    batch, seq, d = x.shape
    n = batch * seq
    depth = w_ada.shape[0]
    a_width = w_up_a.shape[1]
    b_width = w_up_b.shape[1]
    n_heads = b_width // HEAD_DIM
    tm = 256
    bt = 256

    assert depth == 1, "the combine kernel applies the final norm, so only one layer is supported"
    l = 0
    x2 = x.reshape(n, d)
    mod = _ada(c, w_ada[l], b_ada[l])
    h1 = _norm1(x2, norm1_g[l], mod, seq)
    proj = _inproj(h1, w_in[l], a_width, b_width)
    sgu = _sgu(proj, w_s[l], b_s[l], a_width)
    att = _attention(proj, batch, seq, 2 * a_width, 2 * a_width + b_width,
                     2 * a_width + 2 * b_width, n_heads)
    merged = _merge(sgu, att, w_up_a[l], w_up_b[l], proj,
                    2 * a_width + 3 * b_width, 2 * a_width + 3 * b_width + d)
    x1 = _resid_matmul(merged, w_o[l], x2, mod, 2, seq, "out_proj")

    h2, h2p, ids, wts, ranks, counts = _router(x1, norm2_g[l], mod, w_router[l], router_bias[l], seq)
    offsets, tile_expert, n_used, n_tiles = _moe_tiles(counts[:, 0], n, tm)
    pos = _positions(offsets, ids, ranks)
    pos3 = pos.reshape(K_PAD, n // bt, bt).transpose(1, 0, 2)
    xs = _dispatch(h2p, pos3, jnp.zeros((n_tiles * tm, d // 2), U32), bt)
    act = _expert_up(tile_expert, n_used, xs, w_gate_e[l], w_up_e[l], tm)
    ys = _expert_down(tile_expert, n_used, act, w_down_e[l], tm)
    sact = _shared_up(h2, w_gate_s[l], w_up_s[l])
    x1s = _resid_matmul(sact, w_down_s[l], x1, mod, 5, seq, "shared_down")
    out = _combine(pos3, x1s, wts.T, mod, final_g, ys, seq, bt)
    return out.reshape(batch, seq, d)
```

```python
import functools

import jax
import jax.numpy as jnp
from jax import lax
from jax.experimental import pallas as pl
from jax.experimental.pallas import tpu as pltpu

F32 = jnp.float32
BF16 = jnp.bfloat16
I32 = jnp.int32
U32 = jnp.uint32

EPS = 1e-6
A_GROUPS = 8
A_CHUNK = 128
HEAD_DIM = 128
N_EXPERTS = 64
TOP_K = 6
N_GROUPS = 8
TOPK_GROUPS = 4
ROUTED_SCALE = 2.5
N_MOD = 6
SUBLANES = 8
K_PAD = SUBLANES
DEAD_LOG = -105.0

VMEM_LIMIT = 56 * 1024 * 1024


def _cparams(sem):
    return pltpu.CompilerParams(dimension_semantics=sem, vmem_limit_bytes=VMEM_LIMIT)


def _dot(a, b):
    return jnp.dot(a, b, preferred_element_type=F32)


def _dot_nt(a, b):
    return lax.dot_general(a, b, (((1,), (1,)), ((), ())), preferred_element_type=F32)


def _pack_halves(h):
    c = h.shape[1] // 2
    lo = pltpu.bitcast(h[:, :c].astype(BF16).astype(F32), U32)
    hi = pltpu.bitcast(h[:, c:].astype(BF16).astype(F32), U32)
    return (hi & jnp.uint32(0xFFFF0000)) | (lo >> 16)


def _unpack_halves(p):
    lo = pltpu.bitcast(p << 16, F32)
    hi = pltpu.bitcast(p & jnp.uint32(0xFFFF0000), F32)
    return lo, hi


def _ada_kernel(c_ref, w_ref, b_ref, o_ref):
    c = c_ref[...]
    cact = (c * jax.nn.sigmoid(c)).astype(BF16)
    o_ref[...] = _dot(cact, w_ref[...].astype(BF16)) + b_ref[...]


def _ada(c, w_ada, b_ada, bn=512):
    b, d = c.shape
    n = w_ada.shape[1]
    cp = jnp.zeros((8, d), F32).at[:b].set(c)
    out = pl.pallas_call(
        _ada_kernel,
        grid=(n // bn,),
        in_specs=[
            pl.BlockSpec((8, d), lambda j: (0, 0)),
            pl.BlockSpec((d, bn), lambda j: (0, j)),
            pl.BlockSpec((1, bn), lambda j: (0, j)),
        ],
        out_specs=pl.BlockSpec((8, bn), lambda j: (0, j)),
        out_shape=jax.ShapeDtypeStruct((8, n), F32),
        compiler_params=_cparams(("arbitrary",)),
        name="ada",
    )(cp, w_ada, b_ada.reshape(1, n))
    return out[:b].reshape(b, N_MOD, d)


def _norm_mod(x, g_ref, mod_ref, shift_idx, scale_idx):
    y = x * lax.rsqrt(jnp.mean(x * x, axis=-1, keepdims=True) + EPS) * g_ref[...]
    mod = mod_ref[0]
    return y * (1.0 + mod[scale_idx:scale_idx + 1]) + mod[shift_idx:shift_idx + 1]


def _norm1_kernel(x_ref, g_ref, mod_ref, o_ref):
    o_ref[...] = _norm_mod(x_ref[...], g_ref, mod_ref, 0, 1).astype(BF16)


def _norm1(x2, g, mod, seq, bm=512):
    n, d = x2.shape
    return pl.pallas_call(
        _norm1_kernel,
        grid=(n // bm,),
        in_specs=[
            pl.BlockSpec((bm, d), lambda i: (i, 0)),
            pl.BlockSpec((1, d), lambda i: (0, 0)),
            pl.BlockSpec((1, N_MOD, d), lambda i: (i * bm // seq, 0, 0)),
        ],
        out_specs=pl.BlockSpec((bm, d), lambda i: (i, 0)),
        out_shape=jax.ShapeDtypeStruct((n, d), BF16),
        compiler_params=_cparams(("arbitrary",)),
        name="norm1",
    )(x2, g.reshape(1, d), mod)


def _inproj_kernel(h_ref, w_ref, o_ref, *, n_gelu, n_ln, n_plain, group):
    j = pl.program_id(1)
    acc = _dot(h_ref[...], w_ref[...].astype(BF16))

    @pl.when(j < n_gelu)
    def _():
        o_ref[...] = jax.nn.gelu(acc).astype(o_ref.dtype)

    @pl.when((j >= n_gelu) & (j < n_ln))
    def _():
        a = jax.nn.gelu(acc)
        for s in range(acc.shape[1] // group):
            seg = a[:, s * group:(s + 1) * group]
            mu = jnp.mean(seg, axis=-1, keepdims=True)
            cen = seg - mu
            var = jnp.mean(cen * cen, axis=-1, keepdims=True)
            o_ref[:, s * group:(s + 1) * group] = (cen * lax.rsqrt(var + EPS)).astype(o_ref.dtype)

    @pl.when((j >= n_ln) & (j < n_plain))
    def _():
        o_ref[...] = acc.astype(o_ref.dtype)

    @pl.when(j >= n_plain)
    def _():
        o_ref[...] = jax.nn.sigmoid(acc).astype(o_ref.dtype)


def _inproj(h, w_in, a_width, b_width, bm=1024, bn=512):
    n, d = h.shape
    cols = w_in.shape[1]
    group = a_width // A_GROUPS
    kern = functools.partial(
        _inproj_kernel,
        n_gelu=a_width // bn,
        n_ln=2 * a_width // bn,
        n_plain=(2 * a_width + 3 * b_width) // bn,
        group=group,
    )
    return pl.pallas_call(
        kern,
        grid=(n // bm, cols // bn),
        in_specs=[
            pl.BlockSpec((bm, d), lambda i, j: (i, 0)),
            pl.BlockSpec((d, bn), lambda i, j: (0, j)),
        ],
        out_specs=pl.BlockSpec((bm, bn), lambda i, j: (i, j)),
        out_shape=jax.ShapeDtypeStruct((n, cols), BF16),
        compiler_params=_cparams(("arbitrary", "arbitrary")),
        name="inproj",
    )(h, w_in)


def _sgu_kernel(u_ref, v_ref, ws_ref, bs_ref, o_ref, *, group):
    t = A_CHUNK
    row = lax.broadcasted_iota(I32, (t, t), 0)
    col = lax.broadcasted_iota(I32, (t, t), 1)
    for g in range(A_GROUPS):
        ws = jnp.where(row >= col, ws_ref[g], 0.0).astype(BF16)
        bias = bs_ref[:, g:g + 1]
        for c in range(u_ref.shape[0] // t):
            rs = slice(c * t, (c + 1) * t)
            cs = slice(g * group, (g + 1) * group)
            mixed = _dot(ws, v_ref[rs, cs]) + bias
            o_ref[rs, cs] = (u_ref[rs, cs].astype(F32) * mixed).astype(o_ref.dtype)


def _sgu(proj, w_s, b_s, a_width, bm=512):
    n = proj.shape[0]
    group = a_width // A_GROUPS
    return pl.pallas_call(
        functools.partial(_sgu_kernel, group=group),
        grid=(n // bm,),
        in_specs=[
            pl.BlockSpec((bm, a_width), lambda i: (i, 0)),
            pl.BlockSpec((bm, a_width), lambda i: (i, 1)),
            pl.BlockSpec((A_GROUPS, A_CHUNK, A_CHUNK), lambda i: (0, 0, 0)),
            pl.BlockSpec((A_CHUNK, A_GROUPS), lambda i: (0, 0)),
        ],
        out_specs=pl.BlockSpec((bm, a_width), lambda i: (i, 0)),
        out_shape=jax.ShapeDtypeStruct((n, a_width), BF16),
        compiler_params=_cparams(("arbitrary",)),
        name="sgu",
    )(proj, proj, w_s, b_s.T)


def _attn_kernel(q_ref, k_ref, v_ref, o_ref, *, blk, heads, scale):
    seq = q_ref.shape[0]
    row = lax.broadcasted_iota(I32, (blk, blk), 0)
    col = lax.broadcasted_iota(I32, (blk, blk), 1)
    upper = jnp.where(row > col, 1.0, 0.0).astype(BF16)
    strict = col < row

    def update(q, ks, hs, c, acc, diagonal):
        k = k_ref[pl.ds(ks, blk), hs]
        v = v_ref[pl.ds(ks, blk), hs]
        z = _dot_nt(q, k) * scale
        sp = jnp.maximum(z, 0.0) + jnp.log(1.0 + jnp.exp(-jnp.abs(z)))
        log_stay = jnp.where(strict, -sp, 0.0) if diagonal else -sp
        hi = log_stay.astype(BF16)
        lo = (log_stay - hi.astype(F32)).astype(BF16)
        later = _dot(hi, upper) + _dot(lo, upper)
        w = jnp.exp((z - sp) + later + c)
        if diagonal:
            w = jnp.where(strict, w, 0.0)
        acc = acc + _dot(w.astype(BF16), v)
        return c + later[:, :1] + log_stay[:, :1], acc

    def q_block(qi, _):
        qs = pl.multiple_of(qi * blk, blk)
        hslices = [slice(h * HEAD_DIM, (h + 1) * HEAD_DIM) for h in range(heads)]
        qv = [q_ref[pl.ds(qs, blk), hs] for hs in hslices]
        state = [update(qv[h], qs, hslices[h], jnp.zeros((blk, 1), F32),
                        jnp.zeros((blk, HEAD_DIM), F32), True) for h in range(heads)]
        cs = tuple(s[0] for s in state)
        accs = tuple(s[1] for s in state)

        def live(carry):
            n, cs, _ = carry
            cmax = jnp.max(cs[0])
            for c in cs[1:]:
                cmax = jnp.maximum(cmax, jnp.max(c))
            return (n <= qi) & (cmax > DEAD_LOG)

        def older_block(carry):
            n, cs, accs = carry
            ks = pl.multiple_of((qi - n) * blk, blk)
            out = [update(qv[h], ks, hslices[h], cs[h], accs[h], False) for h in range(heads)]
            return n + 1, tuple(o[0] for o in out), tuple(o[1] for o in out)

        _, _, accs = lax.while_loop(live, older_block, (jnp.int32(1), cs, accs))
        for h in range(heads):
            o_ref[pl.ds(qs, blk), hslices[h]] = accs[h].astype(o_ref.dtype)
        return 0

    lax.fori_loop(0, seq // blk, q_block, 0)


def _attention(proj, batch, seq, q_col, k_col, v_col, n_heads, blk=256, heads=2):
    width = heads * HEAD_DIM
    qc, kc, vc = q_col // width, k_col // width, v_col // width
    return pl.pallas_call(
        functools.partial(_attn_kernel, blk=blk, heads=heads, scale=HEAD_DIM ** -0.5),
        grid=(batch, n_heads // heads),
        in_specs=[
            pl.BlockSpec((seq, width), lambda b, h: (b, qc + h)),
            pl.BlockSpec((seq, width), lambda b, h: (b, kc + h)),
            pl.BlockSpec((seq, width), lambda b, h: (b, vc + h)),
        ],
        out_specs=pl.BlockSpec((seq, width), lambda b, h: (b, h)),
        out_shape=jax.ShapeDtypeStruct((batch * seq, n_heads * HEAD_DIM), BF16),
        compiler_params=_cparams(("arbitrary", "arbitrary")),
        name="attn",
    )(proj, proj, proj)


def _merge_kernel(a_ref, b_ref, wa_ref, wb_ref, ga_ref, gb_ref, o_ref):
    ya = _dot(a_ref[...], wa_ref[...].astype(BF16))
    yb = _dot(b_ref[...], wb_ref[...].astype(BF16))
    o_ref[...] = (ga_ref[...].astype(F32) * ya + gb_ref[...].astype(F32) * yb).astype(o_ref.dtype)


def _merge(sgu, att, w_up_a, w_up_b, proj, ga_col, gb_col, bm=1024, bn=512):
    n, ka = sgu.shape
    kb = att.shape[1]
    d = w_up_a.shape[1]
    gac, gbc = ga_col // bn, gb_col // bn
    return pl.pallas_call(
        _merge_kernel,
        grid=(n // bm, d // bn),
        in_specs=[
            pl.BlockSpec((bm, ka), lambda i, j: (i, 0)),
            pl.BlockSpec((bm, kb), lambda i, j: (i, 0)),
            pl.BlockSpec((ka, bn), lambda i, j: (0, j)),
            pl.BlockSpec((kb, bn), lambda i, j: (0, j)),
            pl.BlockSpec((bm, bn), lambda i, j: (i, gac + j)),
            pl.BlockSpec((bm, bn), lambda i, j: (i, gbc + j)),
        ],
        out_specs=pl.BlockSpec((bm, bn), lambda i, j: (i, j)),
        out_shape=jax.ShapeDtypeStruct((n, d), BF16),
        compiler_params=_cparams(("arbitrary", "arbitrary")),
        name="merge",
    )(sgu, att, w_up_a, w_up_b, proj, proj)


def _resid_kernel(a_ref, w_ref, x_ref, mod_ref, o_ref, *, gate_idx):
    y = _dot(a_ref[...], w_ref[...].astype(BF16))
    gate = mod_ref[0][gate_idx:gate_idx + 1]
    o_ref[...] = x_ref[...] + gate * y


def _resid_matmul(a, w, x2, mod, gate_idx, seq, name, bm=1024, bn=512):
    n, k = a.shape
    d = w.shape[1]
    return pl.pallas_call(
        functools.partial(_resid_kernel, gate_idx=gate_idx),
        grid=(n // bm, d // bn),
        in_specs=[
            pl.BlockSpec((bm, k), lambda i, j: (i, 0)),
            pl.BlockSpec((k, bn), lambda i, j: (0, j)),
            pl.BlockSpec((bm, bn), lambda i, j: (i, j)),
            pl.BlockSpec((1, N_MOD, bn), lambda i, j: (i * bm // seq, 0, j)),
        ],
        out_specs=pl.BlockSpec((bm, bn), lambda i, j: (i, j)),
        out_shape=jax.ShapeDtypeStruct((n, d), F32),
        compiler_params=_cparams(("arbitrary", "arbitrary")),
        name=name,
    )(a, w, x2, mod)


def _first_argmax(vals, iota, size):
    m = jnp.max(vals, axis=0, keepdims=True)
    idx = jnp.min(jnp.where(vals == m, iota, size), axis=0, keepdims=True)
    return m, idx


def _router_kernel(x_ref, g_ref, mod_ref, wr_ref, rb_ref, h_ref, hp_ref, ids_ref, wts_ref, rank_ref,
                   cnt_ref, carry_ref):
    i = pl.program_id(0)
    bm = x_ref.shape[0]
    e = N_EXPERTS
    per = e // N_GROUPS

    @pl.when(i == 0)
    def _():
        carry_ref[...] = jnp.zeros_like(carry_ref)

    h = _norm_mod(x_ref[...], g_ref, mod_ref, 3, 4)
    h_ref[...] = h.astype(BF16)
    hp_ref[...] = _pack_halves(h)

    h_hi = h.astype(BF16)
    h_lo = (h - h_hi.astype(F32)).astype(BF16)
    wr = wr_ref[...]
    w_hi = wr.astype(BF16)
    w_lo = (wr - w_hi.astype(F32)).astype(BF16)
    logits = _dot_nt(w_hi, h_hi) + (_dot_nt(w_hi, h_lo) + _dot_nt(w_lo, h_hi))
    scores = jax.nn.sigmoid(logits)
    choice = scores + rb_ref[...]

    iota_p = lax.broadcasted_iota(I32, (per, bm), 0)
    gs_rows = []
    for g in range(N_GROUPS):
        cg = choice[g * per:(g + 1) * per, :]
        m1, i1 = _first_argmax(cg, iota_p, per)
        m2 = jnp.max(jnp.where(iota_p == i1, -jnp.inf, cg), axis=0, keepdims=True)
        gs_rows.append(m1 + m2)
    gs = jnp.concatenate(gs_rows, axis=0)

    iota_g = lax.broadcasted_iota(I32, (N_GROUPS, bm), 0)
    gsel = jnp.zeros((N_GROUPS, bm), F32)
    for _ in range(TOPK_GROUPS):
        _, ig = _first_argmax(gs, iota_g, N_GROUPS)
        hit = iota_g == ig
        gsel = jnp.where(hit, 1.0, gsel)
        gs = jnp.where(hit, -jnp.inf, gs)
    emask = jnp.concatenate(
        [jnp.broadcast_to(gsel[g:g + 1, :], (per, bm)) for g in range(N_GROUPS)], axis=0)

    iota_e = lax.broadcasted_iota(I32, (e, bm), 0)
    masked = jnp.where(emask > 0.5, choice, -jnp.inf)
    self_f = jnp.zeros((e, bm), F32)
    idx_rows, w_rows = [], []
    for _ in range(TOP_K):
        _, ie = _first_argmax(masked, iota_e, e)
        hit = iota_e == ie
        self_f = jnp.where(hit, 1.0, self_f)
        idx_rows.append(ie)
        w_rows.append(jnp.sum(jnp.where(hit, scores, 0.0), axis=0, keepdims=True))
        masked = jnp.where(hit, -jnp.inf, masked)
    w = jnp.concatenate(w_rows, axis=0)
    w = w / jnp.sum(w, axis=0, keepdims=True) * ROUTED_SCALE

    t_row = lax.broadcasted_iota(I32, (bm, bm), 0)
    t_col = lax.broadcasted_iota(I32, (bm, bm), 1)
    before = jnp.where(t_row < t_col, 1.0, 0.0).astype(BF16)
    rank_mat = _dot(self_f.astype(BF16), before) + carry_ref[:, :1]
    rank_rows = [jnp.sum(jnp.where(iota_e == ie, rank_mat, 0.0), axis=0, keepdims=True)
                 for ie in idx_rows]
    carry_ref[...] = carry_ref[...] + jnp.sum(self_f, axis=1, keepdims=True)

    pad_i = jnp.zeros((K_PAD - TOP_K, bm), I32)
    ids_ref[...] = jnp.concatenate(idx_rows + [pad_i], axis=0)
    wts_ref[...] = jnp.concatenate([w, jnp.zeros((K_PAD - TOP_K, bm), F32)], axis=0)
    rank_ref[...] = jnp.concatenate([r.astype(I32) for r in rank_rows] + [pad_i], axis=0)
    cnt_ref[...] = carry_ref[...].astype(I32)


def _router(x1, g, mod, w_router, router_bias, seq, bm=512):
    n, d = x1.shape
    e = N_EXPERTS
    tok = lambda i: (0, i)
    return pl.pallas_call(
        _router_kernel,
        grid=(n // bm,),
        in_specs=[
            pl.BlockSpec((bm, d), lambda i: (i, 0)),
            pl.BlockSpec((1, d), lambda i: (0, 0)),
            pl.BlockSpec((1, N_MOD, d), lambda i: (i * bm // seq, 0, 0)),
            pl.BlockSpec((e, d), lambda i: (0, 0)),
            pl.BlockSpec((e, 1), lambda i: (0, 0)),
        ],
        out_specs=[
            pl.BlockSpec((bm, d), lambda i: (i, 0)),
            pl.BlockSpec((bm, d // 2), lambda i: (i, 0)),
            pl.BlockSpec((K_PAD, bm), tok),
            pl.BlockSpec((K_PAD, bm), tok),
            pl.BlockSpec((K_PAD, bm), tok),
            pl.BlockSpec((e, 128), lambda i: (0, 0)),
        ],
        out_shape=[
            jax.ShapeDtypeStruct((n, d), BF16),
            jax.ShapeDtypeStruct((n, d // 2), U32),
            jax.ShapeDtypeStruct((K_PAD, n), I32),
            jax.ShapeDtypeStruct((K_PAD, n), F32),
            jax.ShapeDtypeStruct((K_PAD, n), I32),
            jax.ShapeDtypeStruct((e, 128), I32),
        ],
        scratch_shapes=[pltpu.VMEM((e, 128), F32)],
        compiler_params=_cparams(("arbitrary",)),
        name="router",
    )(x1, g.reshape(1, d), mod, w_router.T, router_bias.reshape(e, 1))


def _pos_kernel(off_ref, ids_ref, rank_ref, o_ref):
    ids = ids_ref[...]
    pos = rank_ref[...]
    for ex in range(N_EXPERTS):
        pos = pos + jnp.where(ids == ex, off_ref[ex], 0)
    o_ref[...] = pos


def _positions(offsets, ids, ranks):
    return pl.pallas_call(
        _pos_kernel,
        in_specs=[
            pl.BlockSpec(memory_space=pltpu.SMEM),
            pl.BlockSpec(memory_space=pltpu.VMEM),
            pl.BlockSpec(memory_space=pltpu.VMEM),
        ],
        out_specs=pl.BlockSpec(memory_space=pltpu.VMEM),
        out_shape=jax.ShapeDtypeStruct(ids.shape, I32),
        name="positions",
    )(offsets, ids, ranks)


def _dispatch_kernel(pad_ref, pos_ref, h_ref, xs_ref, zeros, sem, pad_sem, *, tm):
    bt = h_ref.shape[0]

    @pl.when(pl.program_id(0) == 0)
    def _():
        zeros[...] = jnp.zeros_like(zeros)
        group_bits = [1 << b for b in reversed(range((tm // SUBLANES - 1).bit_length()))]

        def pad_copies(ex):
            start, gap = pad_ref[0, ex], pad_ref[1, ex]
            end = start + gap
            groups, singles = gap >> 3, gap & (SUBLANES - 1)
            out = []
            for j in range(SUBLANES - 1):
                out.append((j < singles, pltpu.make_async_copy(
                    zeros.at[pl.ds(0, 1)], xs_ref.at[pl.ds(start + j, 1)], pad_sem)))
            for bit in group_bits:
                first = pl.multiple_of(end - SUBLANES * (groups & (2 * bit - 1)), SUBLANES)
                out.append(((groups & bit) != 0, pltpu.make_async_copy(
                    zeros.at[pl.ds(0, SUBLANES * bit)], xs_ref.at[pl.ds(first, SUBLANES * bit)], pad_sem)))
            return out

        def issue_pad(ex, _):
            for cond, cp in pad_copies(ex):
                @pl.when(cond)
                def _():
                    cp.start()
            return 0

        def drain_pad(ex, _):
            for cond, cp in pad_copies(ex):
                @pl.when(cond)
                def _():
                    cp.wait()
            return 0

        lax.fori_loop(0, N_EXPERTS, issue_pad, 0)
        lax.fori_loop(0, N_EXPERTS, drain_pad, 0)

    def copy(r, k):
        p = pos_ref[0, k, r]
        return pltpu.make_async_copy(h_ref.at[pl.ds(r, 1)], xs_ref.at[pl.ds(p, 1)], sem)

    def issue(r, _):
        for k in range(TOP_K):
            copy(r, k).start()
        return 0

    def drain(r, _):
        for k in range(TOP_K):
            copy(r, k).wait()
        return 0

    lax.fori_loop(0, bt, issue, 0)
    lax.fori_loop(0, bt, drain, 0)


def _dispatch(hp, pos3, pads, n_rows, bt, tm):
    n, c = hp.shape
    return pl.pallas_call(
        functools.partial(_dispatch_kernel, tm=tm),
        grid=(n // bt,),
        in_specs=[
            pl.BlockSpec(memory_space=pltpu.SMEM),
            pl.BlockSpec((1, K_PAD, bt), lambda i: (i, 0, 0), memory_space=pltpu.SMEM),
            pl.BlockSpec((bt, c), lambda i: (i, 0)),
        ],
        out_specs=pl.BlockSpec(memory_space=pl.ANY),
        out_shape=jax.ShapeDtypeStruct((n_rows, c), U32),
        scratch_shapes=[pltpu.VMEM((tm // 2, c), U32), pltpu.SemaphoreType.DMA(()),
                        pltpu.SemaphoreType.DMA(())],
        compiler_params=_cparams(("arbitrary",)),
        name="dispatch",
    )(pads, pos3, hp)


def _new_expert(te_ref, t):
    prev = te_ref[jnp.maximum(t - 1, 0)]
    return (t == 0) | (te_ref[t] != prev)


def _expert_up_kernel(te_ref, nu_ref, xs_ref, wg_ref, wu_ref, o_ref, w_scr):
    t = pl.program_id(0)
    f = wg_ref.shape[2]

    @pl.when(_new_expert(te_ref, t))
    def _():
        w_scr[:, :f] = wg_ref[0].astype(BF16)
        w_scr[:, f:] = wu_ref[0].astype(BF16)

    @pl.when(t < nu_ref[0])
    def _():
        lo, hi = _unpack_halves(xs_ref[...])
        x = jnp.concatenate([lo.astype(BF16), hi.astype(BF16)], axis=1)
        gu = _dot(x, w_scr[...])
        g, u = gu[:, :f], gu[:, f:]
        o_ref[...] = (g * jax.nn.sigmoid(g) * u).astype(o_ref.dtype)

    @pl.when(t >= nu_ref[0])
    def _():
        o_ref[...] = jnp.zeros_like(o_ref)


def _expert_up(tile_expert, n_used, xs, w_gate_e, w_up_e, tm):
    p, c = xs.shape
    e, d, f = w_gate_e.shape
    grid_spec = pltpu.PrefetchScalarGridSpec(
        num_scalar_prefetch=2,
        grid=(p // tm,),
        in_specs=[
            pl.BlockSpec((tm, c), lambda t, te, nu: (jnp.minimum(t, nu[0] - 1), 0)),
            pl.BlockSpec((1, d, f), lambda t, te, nu: (te[t], 0, 0)),
            pl.BlockSpec((1, d, f), lambda t, te, nu: (te[t], 0, 0)),
        ],
        out_specs=pl.BlockSpec((tm, f), lambda t, te, nu: (t, 0)),
        scratch_shapes=[pltpu.VMEM((d, 2 * f), BF16)],
    )
    return pl.pallas_call(
        _expert_up_kernel,
        grid_spec=grid_spec,
        out_shape=jax.ShapeDtypeStruct((p, f), BF16),
        compiler_params=_cparams(("arbitrary",)),
        name="expert_up",
    )(tile_expert, n_used, xs, w_gate_e, w_up_e)


def _expert_down_kernel(te_ref, nu_ref, a_ref, wd_ref, o_ref, w_scr):
    t = pl.program_id(0)

    @pl.when(_new_expert(te_ref, t))
    def _():
        w_scr[...] = wd_ref[0].astype(BF16)

    @pl.when(t < nu_ref[0])
    def _():
        o_ref[...] = _pack_halves(_dot(a_ref[...], w_scr[...]))

    @pl.when(t >= nu_ref[0])
    def _():
        o_ref[...] = jnp.zeros_like(o_ref)


def _expert_down(tile_expert, n_used, act, w_down_e, tm):
    p, f = act.shape
    e, _, d = w_down_e.shape
    grid_spec = pltpu.PrefetchScalarGridSpec(
        num_scalar_prefetch=2,
        grid=(p // tm,),
        in_specs=[
            pl.BlockSpec((tm, f), lambda t, te, nu: (jnp.minimum(t, nu[0] - 1), 0)),
            pl.BlockSpec((1, f, d), lambda t, te, nu: (te[t], 0, 0)),
        ],
        out_specs=pl.BlockSpec((tm, d // 2), lambda t, te, nu: (t, 0)),
        scratch_shapes=[pltpu.VMEM((f, d), BF16)],
    )
    return pl.pallas_call(
        _expert_down_kernel,
        grid_spec=grid_spec,
        out_shape=jax.ShapeDtypeStruct((p, d // 2), U32),
        compiler_params=_cparams(("arbitrary",)),
        name="expert_down",
    )(tile_expert, n_used, act, w_down_e)


def _shared_up_kernel(h_ref, wg_ref, wu_ref, o_ref):
    h = h_ref[...]
    g = _dot(h, wg_ref[...].astype(BF16))
    u = _dot(h, wu_ref[...].astype(BF16))
    o_ref[...] = (g * jax.nn.sigmoid(g) * u).astype(o_ref.dtype)


def _shared_up(h, w_gate_s, w_up_s, bm=1024, bn=256):
    n, d = h.shape
    f = w_gate_s.shape[1]
    return pl.pallas_call(
        _shared_up_kernel,
        grid=(n // bm, f // bn),
        in_specs=[
            pl.BlockSpec((bm, d), lambda i, j: (i, 0)),
            pl.BlockSpec((d, bn), lambda i, j: (0, j)),
            pl.BlockSpec((d, bn), lambda i, j: (0, j)),
        ],
        out_specs=pl.BlockSpec((bm, bn), lambda i, j: (i, j)),
        out_shape=jax.ShapeDtypeStruct((n, f), BF16),
        compiler_params=_cparams(("arbitrary", "arbitrary")),
        name="shared_up",
    )(h, w_gate_s, w_up_s)


def _combine_kernel(pos_ref, x_ref, wt_ref, mod_ref, g_ref, ys_ref, o_ref, ybuf, sem):
    bt = x_ref.shape[0]

    def copy(r, k):
        p = pos_ref[0, k, r]
        return pltpu.make_async_copy(ys_ref.at[pl.ds(p, 1)], ybuf.at[k, pl.ds(r, 1)], sem)

    def issue(r, _):
        for k in range(TOP_K):
            copy(r, k).start()
        return 0

    def drain(r, _):
        for k in range(TOP_K):
            copy(r, k).wait()
        return 0

    lax.fori_loop(0, bt, issue, 0)
    lax.fori_loop(0, bt, drain, 0)

    c = ybuf.shape[2]
    acc_lo = jnp.zeros((bt, c), F32)
    acc_hi = jnp.zeros((bt, c), F32)
    for k in range(TOP_K):
        lo, hi = _unpack_halves(ybuf[k])
        wk = wt_ref[:, k:k + 1]
        acc_lo = acc_lo + wk * lo
        acc_hi = acc_hi + wk * hi
    routed = jnp.concatenate([acc_lo, acc_hi], axis=1)
    gate = mod_ref[0][5:6]
    x = x_ref[...] + gate * routed
    y = x * lax.rsqrt(jnp.mean(x * x, axis=-1, keepdims=True) + EPS) * g_ref[...]
    o_ref[...] = y


def _combine(pos3, x1s, wts_t, mod, final_g, ys, seq, bt):
    n, d = x1s.shape
    return pl.pallas_call(
        _combine_kernel,
        grid=(n // bt,),
        in_specs=[
            pl.BlockSpec((1, K_PAD, bt), lambda i: (i, 0, 0), memory_space=pltpu.SMEM),
            pl.BlockSpec((bt, d), lambda i: (i, 0)),
            pl.BlockSpec((bt, K_PAD), lambda i: (i, 0)),
            pl.BlockSpec((1, N_MOD, d), lambda i: (i * bt // seq, 0, 0)),
            pl.BlockSpec((1, d), lambda i: (0, 0)),
            pl.BlockSpec(memory_space=pl.ANY),
        ],
        out_specs=pl.BlockSpec((bt, d), lambda i: (i, 0)),
        out_shape=jax.ShapeDtypeStruct((n, d), F32),
        scratch_shapes=[pltpu.VMEM((TOP_K, bt, d // 2), U32), pltpu.SemaphoreType.DMA(())],
        compiler_params=_cparams(("arbitrary",)),
        name="combine",
    )(pos3, x1s, wts_t, mod, final_g.reshape(1, d), ys)


def _moe_tiles(counts, n_tokens, tm):
    e = counts.shape[0]
    n_tiles = n_tokens * TOP_K // tm + e
    tiles_per = (counts + tm - 1) // tm
    tile_end = jnp.cumsum(tiles_per)
    offsets = (tile_end - tiles_per) * tm
    n_used = tile_end[-1]
    t = jnp.minimum(jnp.arange(n_tiles, dtype=I32), n_used - 1)
    te = jnp.sum((t[:, None] >= tile_end[None, :]).astype(I32), axis=1)
    pads = jnp.stack([offsets + counts, tiles_per * tm - counts]).astype(I32)
    return offsets.astype(I32), te, n_used.reshape(1).astype(I32), pads, n_tiles


def kernel(x, c, w_ada, b_ada, norm1_g, w_in, w_s, b_s, w_up_a, w_up_b, w_o, norm2_g, w_router,
           router_bias, w_gate_e, w_up_e, w_down_e, w_gate_s, w_up_s, w_down_s, final_g):
    batch, seq, d = x.shape
    n = batch * seq
    depth = w_ada.shape[0]
    a_width = w_up_a.shape[1]
    b_width = w_up_b.shape[1]
    n_heads = b_width // HEAD_DIM
    tm = 256
    bt = 256

    assert depth == 1, "the combine kernel applies the final norm, so only one layer is supported"
    l = 0
    x2 = x.reshape(n, d)
    mod = _ada(c, w_ada[l], b_ada[l])
    h1 = _norm1(x2, norm1_g[l], mod, seq)
    proj = _inproj(h1, w_in[l], a_width, b_width)
    sgu = _sgu(proj, w_s[l], b_s[l], a_width)
    att = _attention(proj, batch, seq, 2 * a_width, 2 * a_width + b_width,
                     2 * a_width + 2 * b_width, n_heads)
    merged = _merge(sgu, att, w_up_a[l], w_up_b[l], proj,
                    2 * a_width + 3 * b_width, 2 * a_width + 3 * b_width + d)
    x1 = _resid_matmul(merged, w_o[l], x2, mod, 2, seq, "out_proj")

    h2, h2p, ids, wts, ranks, counts = _router(x1, norm2_g[l], mod, w_router[l], router_bias[l], seq)
    offsets, tile_expert, n_used, pads, n_tiles = _moe_tiles(counts[:, 0], n, tm)
    pos = _positions(offsets, ids, ranks)
    pos3 = pos.reshape(K_PAD, n // bt, bt).transpose(1, 0, 2)
    xs = _dispatch(h2p, pos3, pads, n_tiles * tm, bt, tm)
    act = _expert_up(tile_expert, n_used, xs, w_gate_e[l], w_up_e[l], tm)
    ys = _expert_down(tile_expert, n_used, act, w_down_e[l], tm)
    sact = _shared_up(h2, w_gate_s[l], w_up_s[l])
    x1s = _resid_matmul(sact, w_down_s[l], x1, mod, 5, seq, "shared_down")
    out = _combine(pos3, x1s, wts.T, mod, final_g, ys, seq, bt)
    return out.reshape(batch, seq, d)
```

```python
import functools

import jax
import jax.numpy as jnp
from jax import lax
from jax.experimental import pallas as pl
from jax.experimental.pallas import tpu as pltpu

F32 = jnp.float32
BF16 = jnp.bfloat16
I32 = jnp.int32
U32 = jnp.uint32

EPS = 1e-6
A_GROUPS = 8
A_CHUNK = 128
HEAD_DIM = 128
N_EXPERTS = 64
TOP_K = 6
N_GROUPS = 8
TOPK_GROUPS = 4
ROUTED_SCALE = 2.5
N_MOD = 6
SUBLANES = 8
K_PAD = SUBLANES
ROW_DMA_UNROLL = 8
DEAD_LOG = -105.0

VMEM_LIMIT = 56 * 1024 * 1024


def _cparams(sem):
    return pltpu.CompilerParams(dimension_semantics=sem, vmem_limit_bytes=VMEM_LIMIT)


def _dot(a, b):
    return jnp.dot(a, b, preferred_element_type=F32)


def _dot_nt(a, b):
    return lax.dot_general(a, b, (((1,), (1,)), ((), ())), preferred_element_type=F32)


def _pack_halves(h):
    c = h.shape[1] // 2
    lo = pltpu.bitcast(h[:, :c].astype(BF16).astype(F32), U32)
    hi = pltpu.bitcast(h[:, c:].astype(BF16).astype(F32), U32)
    return (hi & jnp.uint32(0xFFFF0000)) | (lo >> 16)


def _unpack_halves(p):
    lo = pltpu.bitcast(p << 16, F32)
    hi = pltpu.bitcast(p & jnp.uint32(0xFFFF0000), F32)
    return lo, hi


def _ada_kernel(c_ref, w_ref, b_ref, o_ref):
    c = c_ref[...]
    cact = (c * jax.nn.sigmoid(c)).astype(BF16)
    o_ref[...] = _dot(cact, w_ref[...].astype(BF16)) + b_ref[...]


def _ada(c, w_ada, b_ada, bn=512):
    b, d = c.shape
    n = w_ada.shape[1]
    cp = jnp.zeros((8, d), F32).at[:b].set(c)
    out = pl.pallas_call(
        _ada_kernel,
        grid=(n // bn,),
        in_specs=[
            pl.BlockSpec((8, d), lambda j: (0, 0)),
            pl.BlockSpec((d, bn), lambda j: (0, j)),
            pl.BlockSpec((1, bn), lambda j: (0, j)),
        ],
        out_specs=pl.BlockSpec((8, bn), lambda j: (0, j)),
        out_shape=jax.ShapeDtypeStruct((8, n), F32),
        compiler_params=_cparams(("arbitrary",)),
        name="ada",
    )(cp, w_ada, b_ada.reshape(1, n))
    return out[:b].reshape(b, N_MOD, d)


def _norm_mod(x, g_ref, mod_ref, shift_idx, scale_idx):
    y = x * lax.rsqrt(jnp.mean(x * x, axis=-1, keepdims=True) + EPS) * g_ref[...]
    mod = mod_ref[0]
    return y * (1.0 + mod[scale_idx:scale_idx + 1]) + mod[shift_idx:shift_idx + 1]


def _norm1_kernel(x_ref, g_ref, mod_ref, o_ref):
    o_ref[...] = _norm_mod(x_ref[...], g_ref, mod_ref, 0, 1).astype(BF16)


def _norm1(x2, g, mod, seq, bm=512):
    n, d = x2.shape
    return pl.pallas_call(
        _norm1_kernel,
        grid=(n // bm,),
        in_specs=[
            pl.BlockSpec((bm, d), lambda i: (i, 0)),
            pl.BlockSpec((1, d), lambda i: (0, 0)),
            pl.BlockSpec((1, N_MOD, d), lambda i: (i * bm // seq, 0, 0)),
        ],
        out_specs=pl.BlockSpec((bm, d), lambda i: (i, 0)),
        out_shape=jax.ShapeDtypeStruct((n, d), BF16),
        compiler_params=_cparams(("arbitrary",)),
        name="norm1",
    )(x2, g.reshape(1, d), mod)


def _inproj_kernel(h_ref, w_ref, o_ref, *, n_gelu, n_ln, n_plain, group):
    j = pl.program_id(1)
    acc = _dot(h_ref[...], w_ref[...].astype(BF16))

    @pl.when(j < n_gelu)
    def _():
        o_ref[...] = jax.nn.gelu(acc).astype(o_ref.dtype)

    @pl.when((j >= n_gelu) & (j < n_ln))
    def _():
        a = jax.nn.gelu(acc)
        for s in range(acc.shape[1] // group):
            seg = a[:, s * group:(s + 1) * group]
            mu = jnp.mean(seg, axis=-1, keepdims=True)
            cen = seg - mu
            var = jnp.mean(cen * cen, axis=-1, keepdims=True)
            o_ref[:, s * group:(s + 1) * group] = (cen * lax.rsqrt(var + EPS)).astype(o_ref.dtype)

    @pl.when((j >= n_ln) & (j < n_plain))
    def _():
        o_ref[...] = acc.astype(o_ref.dtype)

    @pl.when(j >= n_plain)
    def _():
        o_ref[...] = jax.nn.sigmoid(acc).astype(o_ref.dtype)


def _inproj(h, w_in, a_width, b_width, bm=1024, bn=512):
    n, d = h.shape
    cols = w_in.shape[1]
    group = a_width // A_GROUPS
    kern = functools.partial(
        _inproj_kernel,
        n_gelu=a_width // bn,
        n_ln=2 * a_width // bn,
        n_plain=(2 * a_width + 3 * b_width) // bn,
        group=group,
    )
    return pl.pallas_call(
        kern,
        grid=(n // bm, cols // bn),
        in_specs=[
            pl.BlockSpec((bm, d), lambda i, j: (i, 0)),
            pl.BlockSpec((d, bn), lambda i, j: (0, j)),
        ],
        out_specs=pl.BlockSpec((bm, bn), lambda i, j: (i, j)),
        out_shape=jax.ShapeDtypeStruct((n, cols), BF16),
        compiler_params=_cparams(("arbitrary", "arbitrary")),
        name="inproj",
    )(h, w_in)


def _sgu_kernel(u_ref, v_ref, ws_ref, bs_ref, o_ref, *, group):
    t = A_CHUNK
    row = lax.broadcasted_iota(I32, (t, t), 0)
    col = lax.broadcasted_iota(I32, (t, t), 1)
    for g in range(A_GROUPS):
        ws = jnp.where(row >= col, ws_ref[g], 0.0).astype(BF16)
        bias = bs_ref[:, g:g + 1]
        for c in range(u_ref.shape[0] // t):
            rs = slice(c * t, (c + 1) * t)
            cs = slice(g * group, (g + 1) * group)
            mixed = _dot(ws, v_ref[rs, cs]) + bias
            o_ref[rs, cs] = (u_ref[rs, cs].astype(F32) * mixed).astype(o_ref.dtype)


def _sgu(proj, w_s, b_s, a_width, bm=512):
    n = proj.shape[0]
    group = a_width // A_GROUPS
    return pl.pallas_call(
        functools.partial(_sgu_kernel, group=group),
        grid=(n // bm,),
        in_specs=[
            pl.BlockSpec((bm, a_width), lambda i: (i, 0)),
            pl.BlockSpec((bm, a_width), lambda i: (i, 1)),
            pl.BlockSpec((A_GROUPS, A_CHUNK, A_CHUNK), lambda i: (0, 0, 0)),
            pl.BlockSpec((A_CHUNK, A_GROUPS), lambda i: (0, 0)),
        ],
        out_specs=pl.BlockSpec((bm, a_width), lambda i: (i, 0)),
        out_shape=jax.ShapeDtypeStruct((n, a_width), BF16),
        compiler_params=_cparams(("arbitrary",)),
        name="sgu",
    )(proj, proj, w_s, b_s.T)


def _attn_kernel(q_ref, k_ref, v_ref, o_ref, *, blk, heads, scale):
    seq = q_ref.shape[0]
    row = lax.broadcasted_iota(I32, (blk, blk), 0)
    col = lax.broadcasted_iota(I32, (blk, blk), 1)
    upper = jnp.where(row > col, 1.0, 0.0).astype(BF16)
    strict = col < row

    hslices = [slice(h * HEAD_DIM, (h + 1) * HEAD_DIM) for h in range(heads)]

    def update(qv, ks, cs, accs, diagonal):
        zs = [_dot_nt(qv[h], k_ref[pl.ds(ks, blk), hslices[h]]) * scale for h in range(heads)]
        sps = [jnp.maximum(z, 0.0) + jnp.log(1.0 + jnp.exp(-jnp.abs(z))) for z in zs]
        stays = [jnp.where(strict, -sp, 0.0) if diagonal else -sp for sp in sps]
        his = [s.astype(BF16) for s in stays]
        los = [(s - hi.astype(F32)).astype(BF16) for s, hi in zip(stays, his)]
        laters = [_dot(hi, upper) + _dot(lo, upper) for hi, lo in zip(his, los)]
        ws = [jnp.exp((zs[h] - sps[h]) + laters[h] + cs[h]) for h in range(heads)]
        if diagonal:
            ws = [jnp.where(strict, w, 0.0) for w in ws]
        accs = tuple(accs[h] + _dot(ws[h].astype(BF16), v_ref[pl.ds(ks, blk), hslices[h]])
                     for h in range(heads))
        cs = tuple(cs[h] + laters[h][:, :1] + stays[h][:, :1] for h in range(heads))
        return cs, accs

    def q_block(qi, _):
        qs = pl.multiple_of(qi * blk, blk)
        qv = [q_ref[pl.ds(qs, blk), hs] for hs in hslices]
        cs, accs = update(qv, qs, (jnp.zeros((blk, 1), F32),) * heads,
                          (jnp.zeros((blk, HEAD_DIM), F32),) * heads, True)

        def live(carry):
            n, cs, _ = carry
            cmax = jnp.max(cs[0])
            for c in cs[1:]:
                cmax = jnp.maximum(cmax, jnp.max(c))
            return (n <= qi) & (cmax > DEAD_LOG)

        def older_block(carry):
            n, cs, accs = carry
            ks = pl.multiple_of((qi - n) * blk, blk)
            cs, accs = update(qv, ks, cs, accs, False)
            return n + 1, cs, accs

        _, _, accs = lax.while_loop(live, older_block, (jnp.int32(1), cs, accs))
        for h in range(heads):
            o_ref[pl.ds(qs, blk), hslices[h]] = accs[h].astype(o_ref.dtype)
        return 0

    lax.fori_loop(0, seq // blk, q_block, 0)


def _attention(proj, batch, seq, q_col, k_col, v_col, n_heads, blk=256, heads=4):
    width = heads * HEAD_DIM
    qc, kc, vc = q_col // width, k_col // width, v_col // width
    return pl.pallas_call(
        functools.partial(_attn_kernel, blk=blk, heads=heads, scale=HEAD_DIM ** -0.5),
        grid=(batch, n_heads // heads),
        in_specs=[
            pl.BlockSpec((seq, width), lambda b, h: (b, qc + h)),
            pl.BlockSpec((seq, width), lambda b, h: (b, kc + h)),
            pl.BlockSpec((seq, width), lambda b, h: (b, vc + h)),
        ],
        out_specs=pl.BlockSpec((seq, width), lambda b, h: (b, h)),
        out_shape=jax.ShapeDtypeStruct((batch * seq, n_heads * HEAD_DIM), BF16),
        compiler_params=_cparams(("arbitrary", "arbitrary")),
        name="attn",
    )(proj, proj, proj)


def _merge_kernel(a_ref, b_ref, wa_ref, wb_ref, ga_ref, gb_ref, o_ref):
    ya = _dot(a_ref[...], wa_ref[...].astype(BF16))
    yb = _dot(b_ref[...], wb_ref[...].astype(BF16))
    o_ref[...] = (ga_ref[...].astype(F32) * ya + gb_ref[...].astype(F32) * yb).astype(o_ref.dtype)


def _merge(sgu, att, w_up_a, w_up_b, proj, ga_col, gb_col, bm=1024, bn=512):
    n, ka = sgu.shape
    kb = att.shape[1]
    d = w_up_a.shape[1]
    gac, gbc = ga_col // bn, gb_col // bn
    return pl.pallas_call(
        _merge_kernel,
        grid=(n // bm, d // bn),
        in_specs=[
            pl.BlockSpec((bm, ka), lambda i, j: (i, 0)),
            pl.BlockSpec((bm, kb), lambda i, j: (i, 0)),
            pl.BlockSpec((ka, bn), lambda i, j: (0, j)),
            pl.BlockSpec((kb, bn), lambda i, j: (0, j)),
            pl.BlockSpec((bm, bn), lambda i, j: (i, gac + j)),
            pl.BlockSpec((bm, bn), lambda i, j: (i, gbc + j)),
        ],
        out_specs=pl.BlockSpec((bm, bn), lambda i, j: (i, j)),
        out_shape=jax.ShapeDtypeStruct((n, d), BF16),
        compiler_params=_cparams(("arbitrary", "arbitrary")),
        name="merge",
    )(sgu, att, w_up_a, w_up_b, proj, proj)


def _resid_kernel(a_ref, w_ref, x_ref, mod_ref, o_ref, *, gate_idx):
    y = _dot(a_ref[...], w_ref[...].astype(BF16))
    gate = mod_ref[0][gate_idx:gate_idx + 1]
    o_ref[...] = x_ref[...] + gate * y


def _resid_matmul(a, w, x2, mod, gate_idx, seq, name, bm=1024, bn=512):
    n, k = a.shape
    d = w.shape[1]
    return pl.pallas_call(
        functools.partial(_resid_kernel, gate_idx=gate_idx),
        grid=(n // bm, d // bn),
        in_specs=[
            pl.BlockSpec((bm, k), lambda i, j: (i, 0)),
            pl.BlockSpec((k, bn), lambda i, j: (0, j)),
            pl.BlockSpec((bm, bn), lambda i, j: (i, j)),
            pl.BlockSpec((1, N_MOD, bn), lambda i, j: (i * bm // seq, 0, j)),
        ],
        out_specs=pl.BlockSpec((bm, bn), lambda i, j: (i, j)),
        out_shape=jax.ShapeDtypeStruct((n, d), F32),
        compiler_params=_cparams(("arbitrary", "arbitrary")),
        name=name,
    )(a, w, x2, mod)


def _first_argmax(vals, iota, size):
    m = jnp.max(vals, axis=0, keepdims=True)
    idx = jnp.min(jnp.where(vals == m, iota, size), axis=0, keepdims=True)
    return m, idx


def _router_kernel(x_ref, g_ref, mod_ref, wr_ref, rb_ref, h_ref, hp_ref, ids_ref, wts_ref, rank_ref,
                   cnt_ref, carry_ref):
    i = pl.program_id(0)
    bm = x_ref.shape[0]
    e = N_EXPERTS
    per = e // N_GROUPS

    @pl.when(i == 0)
    def _():
        carry_ref[...] = jnp.zeros_like(carry_ref)

    h = _norm_mod(x_ref[...], g_ref, mod_ref, 3, 4)
    h_ref[...] = h.astype(BF16)
    hp_ref[...] = _pack_halves(h)

    h_hi = h.astype(BF16)
    h_lo = (h - h_hi.astype(F32)).astype(BF16)
    wr = wr_ref[...]
    w_hi = wr.astype(BF16)
    w_lo = (wr - w_hi.astype(F32)).astype(BF16)
    logits = _dot_nt(w_hi, h_hi) + (_dot_nt(w_hi, h_lo) + _dot_nt(w_lo, h_hi))
    scores = jax.nn.sigmoid(logits)
    choice = scores + rb_ref[...]

    iota_p = lax.broadcasted_iota(I32, (per, bm), 0)
    gs_rows = []
    for g in range(N_GROUPS):
        cg = choice[g * per:(g + 1) * per, :]
        m1, i1 = _first_argmax(cg, iota_p, per)
        m2 = jnp.max(jnp.where(iota_p == i1, -jnp.inf, cg), axis=0, keepdims=True)
        gs_rows.append(m1 + m2)
    gs = jnp.concatenate(gs_rows, axis=0)

    iota_g = lax.broadcasted_iota(I32, (N_GROUPS, bm), 0)
    gsel = jnp.zeros((N_GROUPS, bm), F32)
    for _ in range(TOPK_GROUPS):
        _, ig = _first_argmax(gs, iota_g, N_GROUPS)
        hit = iota_g == ig
        gsel = jnp.where(hit, 1.0, gsel)
        gs = jnp.where(hit, -jnp.inf, gs)
    emask = jnp.concatenate(
        [jnp.broadcast_to(gsel[g:g + 1, :], (per, bm)) for g in range(N_GROUPS)], axis=0)

    iota_e = lax.broadcasted_iota(I32, (e, bm), 0)
    masked = jnp.where(emask > 0.5, choice, -jnp.inf)
    self_f = jnp.zeros((e, bm), F32)
    idx_rows, w_rows = [], []
    for _ in range(TOP_K):
        _, ie = _first_argmax(masked, iota_e, e)
        hit = iota_e == ie
        self_f = jnp.where(hit, 1.0, self_f)
        idx_rows.append(ie)
        w_rows.append(jnp.sum(jnp.where(hit, scores, 0.0), axis=0, keepdims=True))
        masked = jnp.where(hit, -jnp.inf, masked)
    w = jnp.concatenate(w_rows, axis=0)
    w = w / jnp.sum(w, axis=0, keepdims=True) * ROUTED_SCALE

    t_row = lax.broadcasted_iota(I32, (bm, bm), 0)
    t_col = lax.broadcasted_iota(I32, (bm, bm), 1)
    before = jnp.where(t_row < t_col, 1.0, 0.0).astype(BF16)
    rank_mat = _dot(self_f.astype(BF16), before) + carry_ref[:, :1]
    rank_rows = [jnp.sum(jnp.where(iota_e == ie, rank_mat, 0.0), axis=0, keepdims=True)
                 for ie in idx_rows]
    carry_ref[...] = carry_ref[...] + jnp.sum(self_f, axis=1, keepdims=True)

    pad_i = jnp.zeros((K_PAD - TOP_K, bm), I32)
    ids_ref[...] = jnp.concatenate(idx_rows + [pad_i], axis=0)
    wts_ref[...] = jnp.concatenate([w, jnp.zeros((K_PAD - TOP_K, bm), F32)], axis=0)
    rank_ref[...] = jnp.concatenate([r.astype(I32) for r in rank_rows] + [pad_i], axis=0)
    cnt_ref[...] = carry_ref[...].astype(I32)


def _router(x1, g, mod, w_router, router_bias, seq, bm=512):
    n, d = x1.shape
    e = N_EXPERTS
    tok = lambda i: (0, i)
    return pl.pallas_call(
        _router_kernel,
        grid=(n // bm,),
        in_specs=[
            pl.BlockSpec((bm, d), lambda i: (i, 0)),
            pl.BlockSpec((1, d), lambda i: (0, 0)),
            pl.BlockSpec((1, N_MOD, d), lambda i: (i * bm // seq, 0, 0)),
            pl.BlockSpec((e, d), lambda i: (0, 0)),
            pl.BlockSpec((e, 1), lambda i: (0, 0)),
        ],
        out_specs=[
            pl.BlockSpec((bm, d), lambda i: (i, 0)),
            pl.BlockSpec((bm, d // 2), lambda i: (i, 0)),
            pl.BlockSpec((K_PAD, bm), tok),
            pl.BlockSpec((K_PAD, bm), tok),
            pl.BlockSpec((K_PAD, bm), tok),
            pl.BlockSpec((e, 128), lambda i: (0, 0)),
        ],
        out_shape=[
            jax.ShapeDtypeStruct((n, d), BF16),
            jax.ShapeDtypeStruct((n, d // 2), U32),
            jax.ShapeDtypeStruct((K_PAD, n), I32),
            jax.ShapeDtypeStruct((K_PAD, n), F32),
            jax.ShapeDtypeStruct((K_PAD, n), I32),
            jax.ShapeDtypeStruct((e, 128), I32),
        ],
        scratch_shapes=[pltpu.VMEM((e, 128), F32)],
        compiler_params=_cparams(("arbitrary",)),
        name="router",
    )(x1, g.reshape(1, d), mod, w_router.T, router_bias.reshape(e, 1))


def _pos_kernel(off_ref, ids_ref, rank_ref, o_ref):
    ids = ids_ref[...]
    pos = rank_ref[...]
    for ex in range(N_EXPERTS):
        pos = pos + jnp.where(ids == ex, off_ref[ex], 0)
    o_ref[...] = pos


def _positions(offsets, ids, ranks):
    return pl.pallas_call(
        _pos_kernel,
        in_specs=[
            pl.BlockSpec(memory_space=pltpu.SMEM),
            pl.BlockSpec(memory_space=pltpu.VMEM),
            pl.BlockSpec(memory_space=pltpu.VMEM),
        ],
        out_specs=pl.BlockSpec(memory_space=pltpu.VMEM),
        out_shape=jax.ShapeDtypeStruct(ids.shape, I32),
        name="positions",
    )(offsets, ids, ranks)


def _dispatch_kernel(pad_ref, pos_ref, h_ref, xs_ref, zeros, sem, pad_sem, *, tm):
    bt = h_ref.shape[0]

    @pl.when(pl.program_id(0) == 0)
    def _():
        zeros[...] = jnp.zeros_like(zeros)
        group_bits = [1 << b for b in reversed(range((tm // SUBLANES - 1).bit_length()))]

        def pad_copies(ex):
            start, gap = pad_ref[0, ex], pad_ref[1, ex]
            end = start + gap
            groups, singles = gap >> 3, gap & (SUBLANES - 1)
            out = []
            for j in range(SUBLANES - 1):
                out.append((j < singles, pltpu.make_async_copy(
                    zeros.at[pl.ds(0, 1)], xs_ref.at[pl.ds(start + j, 1)], pad_sem)))
            for bit in group_bits:
                first = pl.multiple_of(end - SUBLANES * (groups & (2 * bit - 1)), SUBLANES)
                out.append(((groups & bit) != 0, pltpu.make_async_copy(
                    zeros.at[pl.ds(0, SUBLANES * bit)], xs_ref.at[pl.ds(first, SUBLANES * bit)], pad_sem)))
            return out

        def issue_pad(ex, _):
            for cond, cp in pad_copies(ex):
                @pl.when(cond)
                def _():
                    cp.start()
            return 0

        def drain_pad(ex, _):
            for cond, cp in pad_copies(ex):
                @pl.when(cond)
                def _():
                    cp.wait()
            return 0

        lax.fori_loop(0, N_EXPERTS, issue_pad, 0)
        lax.fori_loop(0, N_EXPERTS, drain_pad, 0)

        zrows = zeros.shape[0]
        first = (pad_ref[0, N_EXPERTS - 1] + pad_ref[1, N_EXPERTS - 1]) // zrows

        def tail_copy(g):
            return pltpu.make_async_copy(zeros, xs_ref.at[pl.ds(pl.multiple_of(g * zrows, zrows), zrows)],
                                         pad_sem)

        def issue_tail(g, _):
            tail_copy(g).start()
            return 0

        def drain_tail(g, _):
            tail_copy(g).wait()
            return 0

        lax.fori_loop(first, xs_ref.shape[0] // zrows, issue_tail, 0)
        lax.fori_loop(first, xs_ref.shape[0] // zrows, drain_tail, 0)

    def issue(r, _):
        for k in range(TOP_K):
            p = pos_ref[0, k, r]
            pltpu.make_async_copy(h_ref.at[pl.ds(r, 1)], xs_ref.at[pl.ds(p, 1)], sem).start()
        return 0

    lax.fori_loop(0, bt, issue, 0, unroll=ROW_DMA_UNROLL)
    for _ in range(TOP_K):
        pltpu.make_async_copy(h_ref, xs_ref.at[pl.ds(0, bt)], sem).wait()


def _dispatch(hp, pos3, pads, n_rows, bt, tm):
    n, c = hp.shape
    return pl.pallas_call(
        functools.partial(_dispatch_kernel, tm=tm),
        grid=(n // bt,),
        in_specs=[
            pl.BlockSpec(memory_space=pltpu.SMEM),
            pl.BlockSpec((1, K_PAD, bt), lambda i: (i, 0, 0), memory_space=pltpu.SMEM),
            pl.BlockSpec((bt, c), lambda i: (i, 0)),
        ],
        out_specs=pl.BlockSpec(memory_space=pl.ANY),
        out_shape=jax.ShapeDtypeStruct((n_rows, c), U32),
        scratch_shapes=[pltpu.VMEM((tm // 2, c), U32), pltpu.SemaphoreType.DMA(()),
                        pltpu.SemaphoreType.DMA(())],
        compiler_params=_cparams(("arbitrary",)),
        name="dispatch",
    )(pads, pos3, hp)


def _expert_tiles(ts_ref, src_ref, dst_ref, in_buf, out_buf, zero_buf, in_sem, out_sem, zero_sem, compute):
    e = pl.program_id(0)
    n_exp = pl.num_programs(0)
    tm = in_buf.shape[1]
    n_tiles = dst_ref.shape[0] // tm
    n_used = ts_ref[n_exp]

    def rows(g):
        return pl.ds(pl.multiple_of(g * tm, tm), tm)

    def load(g):
        return pltpu.make_async_copy(src_ref.at[rows(g)], in_buf.at[g % 2], in_sem.at[g % 2])

    def store(g):
        return pltpu.make_async_copy(out_buf.at[g % 2], dst_ref.at[rows(g)], out_sem.at[g % 2])

    @pl.when(e == 0)
    def _():
        load(0).start()

    def tile(g, _):
        @pl.when(g + 1 < n_used)
        def _():
            load(g + 1).start()

        load(g).wait()

        @pl.when(g >= 2)
        def _():
            store(g - 2).wait()

        out_buf[g % 2] = compute(in_buf[g % 2])
        store(g).start()
        return 0

    lax.fori_loop(ts_ref[e], ts_ref[e + 1], tile, 0)

    @pl.when(e == n_exp - 1)
    def _():
        @pl.when(n_used >= 2)
        def _():
            store(n_used - 2).wait()

        store(n_used - 1).wait()

        zero_buf[...] = jnp.zeros_like(zero_buf)

        def fill(g):
            return pltpu.make_async_copy(zero_buf, dst_ref.at[rows(g)], zero_sem)

        def start_fill(g, _):
            fill(g).start()
            return 0

        def wait_fill(g, _):
            fill(g).wait()
            return 0

        lax.fori_loop(n_used, n_tiles, start_fill, 0)
        lax.fori_loop(n_used, n_tiles, wait_fill, 0)


def _expert_up_kernel(ts_ref, wg_ref, wu_ref, xs_ref, act_ref, w_scr, in_buf, out_buf, zero_buf,
                      in_sem, out_sem, zero_sem):
    e = pl.program_id(0)
    f = wg_ref.shape[2]

    @pl.when(ts_ref[e + 1] > ts_ref[e])
    def _():
        w_scr[:, :f] = wg_ref[0].astype(BF16)
        w_scr[:, f:] = wu_ref[0].astype(BF16)

    def compute(xp):
        lo, hi = _unpack_halves(xp)
        x = jnp.concatenate([lo.astype(BF16), hi.astype(BF16)], axis=1)
        gu = _dot(x, w_scr[...])
        g, u = gu[:, :f], gu[:, f:]
        return (g * jax.nn.sigmoid(g) * u).astype(BF16)

    _expert_tiles(ts_ref, xs_ref, act_ref, in_buf, out_buf, zero_buf, in_sem, out_sem, zero_sem, compute)


def _expert_up(tile_start, xs, w_gate_e, w_up_e, tm):
    p, c = xs.shape
    e, d, f = w_gate_e.shape
    grid_spec = pltpu.PrefetchScalarGridSpec(
        num_scalar_prefetch=1,
        grid=(e,),
        in_specs=[
            pl.BlockSpec((1, d, f), lambda i, ts: (i, 0, 0)),
            pl.BlockSpec((1, d, f), lambda i, ts: (i, 0, 0)),
            pl.BlockSpec(memory_space=pl.ANY),
        ],
        out_specs=pl.BlockSpec(memory_space=pl.ANY),
        scratch_shapes=[
            pltpu.VMEM((d, 2 * f), BF16),
            pltpu.VMEM((2, tm, c), U32),
            pltpu.VMEM((2, tm, f), BF16),
            pltpu.VMEM((tm, f), BF16),
            pltpu.SemaphoreType.DMA((2,)),
            pltpu.SemaphoreType.DMA((2,)),
            pltpu.SemaphoreType.DMA(()),
        ],
    )
    return pl.pallas_call(
        _expert_up_kernel,
        grid_spec=grid_spec,
        out_shape=jax.ShapeDtypeStruct((p, f), BF16),
        compiler_params=_cparams(("arbitrary",)),
        name="expert_up",
    )(tile_start, w_gate_e, w_up_e, xs)


def _expert_down_kernel(ts_ref, wd_ref, act_ref, ys_ref, w_scr, in_buf, out_buf, zero_buf,
                        in_sem, out_sem, zero_sem):
    e = pl.program_id(0)

    @pl.when(ts_ref[e + 1] > ts_ref[e])
    def _():
        w_scr[...] = wd_ref[0].astype(BF16)

    def compute(a):
        return _pack_halves(_dot(a, w_scr[...]))

    _expert_tiles(ts_ref, act_ref, ys_ref, in_buf, out_buf, zero_buf, in_sem, out_sem, zero_sem, compute)


def _expert_down(tile_start, act, w_down_e, tm):
    p, f = act.shape
    e, _, d = w_down_e.shape
    grid_spec = pltpu.PrefetchScalarGridSpec(
        num_scalar_prefetch=1,
        grid=(e,),
        in_specs=[
            pl.BlockSpec((1, f, d), lambda i, ts: (i, 0, 0)),
            pl.BlockSpec(memory_space=pl.ANY),
        ],
        out_specs=pl.BlockSpec(memory_space=pl.ANY),
        scratch_shapes=[
            pltpu.VMEM((f, d), BF16),
            pltpu.VMEM((2, tm, f), BF16),
            pltpu.VMEM((2, tm, d // 2), U32),
            pltpu.VMEM((tm, d // 2), U32),
            pltpu.SemaphoreType.DMA((2,)),
            pltpu.SemaphoreType.DMA((2,)),
            pltpu.SemaphoreType.DMA(()),
        ],
    )
    return pl.pallas_call(
        _expert_down_kernel,
        grid_spec=grid_spec,
        out_shape=jax.ShapeDtypeStruct((p, d // 2), U32),
        compiler_params=_cparams(("arbitrary",)),
        name="expert_down",
    )(tile_start, w_down_e, act)


def _shared_up_kernel(h_ref, wg_ref, wu_ref, o_ref):
    h = h_ref[...]
    g = _dot(h, wg_ref[...].astype(BF16))
    u = _dot(h, wu_ref[...].astype(BF16))
    o_ref[...] = (g * jax.nn.sigmoid(g) * u).astype(o_ref.dtype)


def _shared_up(h, w_gate_s, w_up_s, bm=1024, bn=256):
    n, d = h.shape
    f = w_gate_s.shape[1]
    return pl.pallas_call(
        _shared_up_kernel,
        grid=(n // bm, f // bn),
        in_specs=[
            pl.BlockSpec((bm, d), lambda i, j: (i, 0)),
            pl.BlockSpec((d, bn), lambda i, j: (0, j)),
            pl.BlockSpec((d, bn), lambda i, j: (0, j)),
        ],
        out_specs=pl.BlockSpec((bm, bn), lambda i, j: (i, j)),
        out_shape=jax.ShapeDtypeStruct((n, f), BF16),
        compiler_params=_cparams(("arbitrary", "arbitrary")),
        name="shared_up",
    )(h, w_gate_s, w_up_s)


def _combine_kernel(pos_ref, pos_next_ref, x_ref, wt_ref, mod_ref, g_ref, ys_ref, o_ref, ybuf, sem):
    i = pl.program_id(0)
    bt = x_ref.shape[0]
    slot = i % 2

    def gather(p_ref, dst_slot):
        def issue(r, _):
            for k in range(TOP_K):
                p = p_ref[0, k, r]
                pltpu.make_async_copy(ys_ref.at[pl.ds(p, 1)], ybuf.at[dst_slot, k, pl.ds(r, 1)],
                                      sem.at[dst_slot]).start()
            return 0

        lax.fori_loop(0, bt, issue, 0, unroll=ROW_DMA_UNROLL)

    @pl.when(i == 0)
    def _():
        gather(pos_ref, 0)

    @pl.when(i + 1 < pl.num_programs(0))
    def _():
        gather(pos_next_ref, 1 - slot)

    for k in range(TOP_K):
        pltpu.make_async_copy(ys_ref.at[pl.ds(0, bt)], ybuf.at[slot, k], sem.at[slot]).wait()

    c = ybuf.shape[3]
    acc_lo = jnp.zeros((bt, c), F32)
    acc_hi = jnp.zeros((bt, c), F32)
    for k in range(TOP_K):
        lo, hi = _unpack_halves(ybuf[slot, k])
        wk = wt_ref[:, k:k + 1]
        acc_lo = acc_lo + wk * lo
        acc_hi = acc_hi + wk * hi
    routed = jnp.concatenate([acc_lo, acc_hi], axis=1)
    gate = mod_ref[0][5:6]
    x = x_ref[...] + gate * routed
    y = x * lax.rsqrt(jnp.mean(x * x, axis=-1, keepdims=True) + EPS) * g_ref[...]
    o_ref[...] = y


def _combine(pos3, x1s, wts_t, mod, final_g, ys, seq, bt):
    n, d = x1s.shape
    nb = n // bt
    pos_spec = lambda f: pl.BlockSpec((1, K_PAD, bt), f, memory_space=pltpu.SMEM)
    return pl.pallas_call(
        _combine_kernel,
        grid=(nb,),
        in_specs=[
            pos_spec(lambda i: (i, 0, 0)),
            pos_spec(lambda i: (jnp.minimum(i + 1, nb - 1), 0, 0)),
            pl.BlockSpec((bt, d), lambda i: (i, 0)),
            pl.BlockSpec((bt, K_PAD), lambda i: (i, 0)),
            pl.BlockSpec((1, N_MOD, d), lambda i: (i * bt // seq, 0, 0)),
            pl.BlockSpec((1, d), lambda i: (0, 0)),
            pl.BlockSpec(memory_space=pl.ANY),
        ],
        out_specs=pl.BlockSpec((bt, d), lambda i: (i, 0)),
        out_shape=jax.ShapeDtypeStruct((n, d), F32),
        scratch_shapes=[pltpu.VMEM((2, TOP_K, bt, d // 2), U32), pltpu.SemaphoreType.DMA((2,))],
        compiler_params=_cparams(("arbitrary",)),
        name="combine",
    )(pos3, pos3, x1s, wts_t, mod, final_g.reshape(1, d), ys)


def _moe_tiles(counts, n_tokens, tm):
    e = counts.shape[0]
    n_tiles = n_tokens * TOP_K // tm + e
    tiles_per = (counts + tm - 1) // tm
    tile_end = jnp.cumsum(tiles_per)
    tile_start = jnp.concatenate([jnp.zeros((1,), I32), tile_end.astype(I32)])
    offsets = tile_start[:-1] * tm
    pads = jnp.stack([offsets + counts, tiles_per * tm - counts]).astype(I32)
    return offsets, tile_start, pads, n_tiles


def kernel(x, c, w_ada, b_ada, norm1_g, w_in, w_s, b_s, w_up_a, w_up_b, w_o, norm2_g, w_router,
           router_bias, w_gate_e, w_up_e, w_down_e, w_gate_s, w_up_s, w_down_s, final_g):
    batch, seq, d = x.shape
    n = batch * seq
    depth = w_ada.shape[0]
    a_width = w_up_a.shape[1]
    b_width = w_up_b.shape[1]
    n_heads = b_width // HEAD_DIM
    tm = 256
    bt = 256

    assert depth == 1, "the combine kernel applies the final norm, so only one layer is supported"
    l = 0
    x2 = x.reshape(n, d)
    mod = _ada(c, w_ada[l], b_ada[l])
    h1 = _norm1(x2, norm1_g[l], mod, seq)
    proj = _inproj(h1, w_in[l], a_width, b_width)
    sgu = _sgu(proj, w_s[l], b_s[l], a_width)
    att = _attention(proj, batch, seq, 2 * a_width, 2 * a_width + b_width,
                     2 * a_width + 2 * b_width, n_heads)
    merged = _merge(sgu, att, w_up_a[l], w_up_b[l], proj,
                    2 * a_width + 3 * b_width, 2 * a_width + 3 * b_width + d)
    x1 = _resid_matmul(merged, w_o[l], x2, mod, 2, seq, "out_proj")

    h2, h2p, ids, wts, ranks, counts = _router(x1, norm2_g[l], mod, w_router[l], router_bias[l], seq)
    offsets, tile_start, pads, n_tiles = _moe_tiles(counts[:, 0], n, tm)
    pos = _positions(offsets, ids, ranks)
    pos3 = pos.reshape(K_PAD, n // bt, bt).transpose(1, 0, 2)
    xs = _dispatch(h2p, pos3, pads, n_tiles * tm, bt, tm)
    act = _expert_up(tile_start, xs, w_gate_e[l], w_up_e[l], tm)
    ys = _expert_down(tile_start, act, w_down_e[l], tm)
    sact = _shared_up(h2, w_gate_s[l], w_up_s[l])
    x1s = _resid_matmul(sact, w_down_s[l], x1, mod, 5, seq, "shared_down")
    out = _combine(pos3, x1s, wts.T, mod, final_g, ys, seq, bt)
    return out.reshape(batch, seq, d)
```

```python
import functools

import jax
import jax.numpy as jnp
from jax import lax
from jax.experimental import pallas as pl
from jax.experimental.pallas import tpu as pltpu

F32 = jnp.float32
BF16 = jnp.bfloat16
I32 = jnp.int32
U32 = jnp.uint32

EPS = 1e-6
A_GROUPS = 8
A_CHUNK = 128
HEAD_DIM = 128
N_EXPERTS = 64
TOP_K = 6
N_GROUPS = 8
TOPK_GROUPS = 4
ROUTED_SCALE = 2.5
N_MOD = 6
SUBLANES = 8
K_PAD = SUBLANES
ROW_DMA_UNROLL = 8
DEAD_LOG = -105.0

VMEM_LIMIT = 56 * 1024 * 1024


def _cparams(sem):
    return pltpu.CompilerParams(dimension_semantics=sem, vmem_limit_bytes=VMEM_LIMIT)


def _dot(a, b):
    return jnp.dot(a, b, preferred_element_type=F32)


def _dot_nt(a, b):
    return lax.dot_general(a, b, (((1,), (1,)), ((), ())), preferred_element_type=F32)


def _pack_halves(h):
    c = h.shape[1] // 2
    lo = pltpu.bitcast(h[:, :c].astype(BF16).astype(F32), U32)
    hi = pltpu.bitcast(h[:, c:].astype(BF16).astype(F32), U32)
    return (hi & jnp.uint32(0xFFFF0000)) | (lo >> 16)


def _unpack_halves(p):
    lo = pltpu.bitcast(p << 16, F32)
    hi = pltpu.bitcast(p & jnp.uint32(0xFFFF0000), F32)
    return lo, hi


def _ada_kernel(c_ref, w_ref, b_ref, o_ref):
    c = c_ref[...]
    cact = (c * jax.nn.sigmoid(c)).astype(BF16)
    o_ref[...] = _dot(cact, w_ref[...].astype(BF16)) + b_ref[...]


def _ada(c, w_ada, b_ada, bn=512):
    b, d = c.shape
    n = w_ada.shape[1]
    cp = jnp.zeros((8, d), F32).at[:b].set(c)
    out = pl.pallas_call(
        _ada_kernel,
        grid=(n // bn,),
        in_specs=[
            pl.BlockSpec((8, d), lambda j: (0, 0)),
            pl.BlockSpec((d, bn), lambda j: (0, j)),
            pl.BlockSpec((1, bn), lambda j: (0, j)),
        ],
        out_specs=pl.BlockSpec((8, bn), lambda j: (0, j)),
        out_shape=jax.ShapeDtypeStruct((8, n), F32),
        compiler_params=_cparams(("arbitrary",)),
        name="ada",
    )(cp, w_ada, b_ada.reshape(1, n))
    return out[:b].reshape(b, N_MOD, d)


def _norm_mod(x, g_ref, mod_ref, shift_idx, scale_idx):
    y = x * lax.rsqrt(jnp.mean(x * x, axis=-1, keepdims=True) + EPS) * g_ref[...]
    mod = mod_ref[0]
    return y * (1.0 + mod[scale_idx:scale_idx + 1]) + mod[shift_idx:shift_idx + 1]


def _norm1_kernel(x_ref, g_ref, mod_ref, o_ref):
    o_ref[...] = _norm_mod(x_ref[...], g_ref, mod_ref, 0, 1).astype(BF16)


def _norm1(x2, g, mod, seq, bm=512):
    n, d = x2.shape
    return pl.pallas_call(
        _norm1_kernel,
        grid=(n // bm,),
        in_specs=[
            pl.BlockSpec((bm, d), lambda i: (i, 0)),
            pl.BlockSpec((1, d), lambda i: (0, 0)),
            pl.BlockSpec((1, N_MOD, d), lambda i: (i * bm // seq, 0, 0)),
        ],
        out_specs=pl.BlockSpec((bm, d), lambda i: (i, 0)),
        out_shape=jax.ShapeDtypeStruct((n, d), BF16),
        compiler_params=_cparams(("arbitrary",)),
        name="norm1",
    )(x2, g.reshape(1, d), mod)


def _inproj_kernel(h_ref, w_ref, o_ref, *, n_gelu, n_ln, n_plain, group):
    j = pl.program_id(1)

    def run(epilogue):
        for s in range(o_ref.shape[1] // group):
            cs = slice(s * group, (s + 1) * group)
            acc = _dot(h_ref[...], w_ref[:, cs].astype(BF16))
            o_ref[:, cs] = epilogue(acc).astype(o_ref.dtype)

    def group_norm(acc):
        a = jax.nn.gelu(acc)
        cen = a - jnp.mean(a, axis=-1, keepdims=True)
        var = jnp.mean(cen * cen, axis=-1, keepdims=True)
        return cen * lax.rsqrt(var + EPS)

    @pl.when(j < n_gelu)
    def _():
        run(jax.nn.gelu)

    @pl.when((j >= n_gelu) & (j < n_ln))
    def _():
        run(group_norm)

    @pl.when((j >= n_ln) & (j < n_plain))
    def _():
        run(lambda acc: acc)

    @pl.when(j >= n_plain)
    def _():
        run(jax.nn.sigmoid)


def _inproj(h, w_in, a_width, b_width, bm=1024, bn=512):
    n, d = h.shape
    cols = w_in.shape[1]
    group = a_width // A_GROUPS
    kern = functools.partial(
        _inproj_kernel,
        n_gelu=a_width // bn,
        n_ln=2 * a_width // bn,
        n_plain=(2 * a_width + 3 * b_width) // bn,
        group=group,
    )
    return pl.pallas_call(
        kern,
        grid=(n // bm, cols // bn),
        in_specs=[
            pl.BlockSpec((bm, d), lambda i, j: (i, 0)),
            pl.BlockSpec((d, bn), lambda i, j: (0, j)),
        ],
        out_specs=pl.BlockSpec((bm, bn), lambda i, j: (i, j)),
        out_shape=jax.ShapeDtypeStruct((n, cols), BF16),
        compiler_params=_cparams(("arbitrary", "arbitrary")),
        name="inproj",
    )(h, w_in)


def _sgu_kernel(u_ref, v_ref, ws_ref, bs_ref, o_ref, *, group):
    t = A_CHUNK
    row = lax.broadcasted_iota(I32, (t, t), 0)
    col = lax.broadcasted_iota(I32, (t, t), 1)
    for g in range(A_GROUPS):
        ws = jnp.where(row >= col, ws_ref[g], 0.0).astype(BF16)
        bias = bs_ref[:, g:g + 1]
        for c in range(u_ref.shape[0] // t):
            rs = slice(c * t, (c + 1) * t)
            cs = slice(g * group, (g + 1) * group)
            mixed = _dot(ws, v_ref[rs, cs]) + bias
            o_ref[rs, cs] = (u_ref[rs, cs].astype(F32) * mixed).astype(o_ref.dtype)


def _sgu(proj, w_s, b_s, a_width, bm=512):
    n = proj.shape[0]
    group = a_width // A_GROUPS
    return pl.pallas_call(
        functools.partial(_sgu_kernel, group=group),
        grid=(n // bm,),
        in_specs=[
            pl.BlockSpec((bm, a_width), lambda i: (i, 0)),
            pl.BlockSpec((bm, a_width), lambda i: (i, 1)),
            pl.BlockSpec((A_GROUPS, A_CHUNK, A_CHUNK), lambda i: (0, 0, 0)),
            pl.BlockSpec((A_CHUNK, A_GROUPS), lambda i: (0, 0)),
        ],
        out_specs=pl.BlockSpec((bm, a_width), lambda i: (i, 0)),
        out_shape=jax.ShapeDtypeStruct((n, a_width), BF16),
        compiler_params=_cparams(("arbitrary",)),
        name="sgu",
    )(proj, proj, w_s, b_s.T)


def _attn_kernel(q_ref, k_ref, v_ref, o_ref, *, blk, heads, scale):
    seq = q_ref.shape[0]
    row = lax.broadcasted_iota(I32, (blk, blk), 0)
    col = lax.broadcasted_iota(I32, (blk, blk), 1)
    upper = jnp.where(row > col, 1.0, 0.0).astype(BF16)
    strict = col < row

    hslices = [slice(h * HEAD_DIM, (h + 1) * HEAD_DIM) for h in range(heads)]

    def update(qv, ks, cs, accs, diagonal):
        zs = [_dot_nt(qv[h], k_ref[pl.ds(ks, blk), hslices[h]]) * scale for h in range(heads)]
        sps = [jnp.maximum(z, 0.0) + jnp.log(1.0 + jnp.exp(-jnp.abs(z))) for z in zs]
        stays = [jnp.where(strict, -sp, 0.0) if diagonal else -sp for sp in sps]
        his = [s.astype(BF16) for s in stays]
        los = [(s - hi.astype(F32)).astype(BF16) for s, hi in zip(stays, his)]
        laters = [_dot(hi, upper) + _dot(lo, upper) for hi, lo in zip(his, los)]
        ws = [jnp.exp((zs[h] - sps[h]) + laters[h] + cs[h]) for h in range(heads)]
        if diagonal:
            ws = [jnp.where(strict, w, 0.0) for w in ws]
        accs = tuple(accs[h] + _dot(ws[h].astype(BF16), v_ref[pl.ds(ks, blk), hslices[h]])
                     for h in range(heads))
        cs = tuple(cs[h] + laters[h][:, :1] + stays[h][:, :1] for h in range(heads))
        return cs, accs

    def q_block(qi, _):
        qs = pl.multiple_of(qi * blk, blk)
        qv = [q_ref[pl.ds(qs, blk), hs] for hs in hslices]
        cs, accs = update(qv, qs, (jnp.zeros((blk, 1), F32),) * heads,
                          (jnp.zeros((blk, HEAD_DIM), F32),) * heads, True)

        def live(carry):
            n, cs, _ = carry
            cmax = jnp.max(cs[0])
            for c in cs[1:]:
                cmax = jnp.maximum(cmax, jnp.max(c))
            return (n <= qi) & (cmax > DEAD_LOG)

        def older_block(carry):
            n, cs, accs = carry
            ks = pl.multiple_of((qi - n) * blk, blk)
            cs, accs = update(qv, ks, cs, accs, False)
            return n + 1, cs, accs

        _, _, accs = lax.while_loop(live, older_block, (jnp.int32(1), cs, accs))
        for h in range(heads):
            o_ref[pl.ds(qs, blk), hslices[h]] = accs[h].astype(o_ref.dtype)
        return 0

    lax.fori_loop(0, seq // blk, q_block, 0)


def _attention(proj, batch, seq, q_col, k_col, v_col, n_heads, blk=256, heads=4):
    width = heads * HEAD_DIM
    qc, kc, vc = q_col // width, k_col // width, v_col // width
    return pl.pallas_call(
        functools.partial(_attn_kernel, blk=blk, heads=heads, scale=HEAD_DIM ** -0.5),
        grid=(batch, n_heads // heads),
        in_specs=[
            pl.BlockSpec((seq, width), lambda b, h: (b, qc + h)),
            pl.BlockSpec((seq, width), lambda b, h: (b, kc + h)),
            pl.BlockSpec((seq, width), lambda b, h: (b, vc + h)),
        ],
        out_specs=pl.BlockSpec((seq, width), lambda b, h: (b, h)),
        out_shape=jax.ShapeDtypeStruct((batch * seq, n_heads * HEAD_DIM), BF16),
        compiler_params=_cparams(("arbitrary", "arbitrary")),
        name="attn",
    )(proj, proj, proj)


def _merge_kernel(a_ref, b_ref, wa_ref, wb_ref, ga_ref, gb_ref, o_ref):
    ya = _dot(a_ref[...], wa_ref[...].astype(BF16))
    yb = _dot(b_ref[...], wb_ref[...].astype(BF16))
    o_ref[...] = (ga_ref[...].astype(F32) * ya + gb_ref[...].astype(F32) * yb).astype(o_ref.dtype)


def _merge(sgu, att, w_up_a, w_up_b, proj, ga_col, gb_col, bm=1024, bn=512):
    n, ka = sgu.shape
    kb = att.shape[1]
    d = w_up_a.shape[1]
    gac, gbc = ga_col // bn, gb_col // bn
    return pl.pallas_call(
        _merge_kernel,
        grid=(n // bm, d // bn),
        in_specs=[
            pl.BlockSpec((bm, ka), lambda i, j: (i, 0)),
            pl.BlockSpec((bm, kb), lambda i, j: (i, 0)),
            pl.BlockSpec((ka, bn), lambda i, j: (0, j)),
            pl.BlockSpec((kb, bn), lambda i, j: (0, j)),
            pl.BlockSpec((bm, bn), lambda i, j: (i, gac + j)),
            pl.BlockSpec((bm, bn), lambda i, j: (i, gbc + j)),
        ],
        out_specs=pl.BlockSpec((bm, bn), lambda i, j: (i, j)),
        out_shape=jax.ShapeDtypeStruct((n, d), BF16),
        compiler_params=_cparams(("arbitrary", "arbitrary")),
        name="merge",
    )(sgu, att, w_up_a, w_up_b, proj, proj)


def _resid_kernel(a_ref, w_ref, x_ref, mod_ref, o_ref, *, gate_idx):
    y = _dot(a_ref[...], w_ref[...].astype(BF16))
    gate = mod_ref[0][gate_idx:gate_idx + 1]
    o_ref[...] = x_ref[...] + gate * y


def _resid_matmul(a, w, x2, mod, gate_idx, seq, name, bm=1024, bn=512):
    n, k = a.shape
    d = w.shape[1]
    return pl.pallas_call(
        functools.partial(_resid_kernel, gate_idx=gate_idx),
        grid=(n // bm, d // bn),
        in_specs=[
            pl.BlockSpec((bm, k), lambda i, j: (i, 0)),
            pl.BlockSpec((k, bn), lambda i, j: (0, j)),
            pl.BlockSpec((bm, bn), lambda i, j: (i, j)),
            pl.BlockSpec((1, N_MOD, bn), lambda i, j: (i * bm // seq, 0, j)),
        ],
        out_specs=pl.BlockSpec((bm, bn), lambda i, j: (i, j)),
        out_shape=jax.ShapeDtypeStruct((n, d), F32),
        compiler_params=_cparams(("arbitrary", "arbitrary")),
        name=name,
    )(a, w, x2, mod)


def _first_argmax(vals, iota, size):
    m = jnp.max(vals, axis=0, keepdims=True)
    idx = jnp.min(jnp.where(vals == m, iota, size), axis=0, keepdims=True)
    return m, idx


def _router_kernel(x_ref, g_ref, mod_ref, wr_ref, rb_ref, h_ref, hp_ref, ids_ref, wts_ref, rank_ref,
                   cnt_ref, carry_ref):
    i = pl.program_id(0)
    bm = x_ref.shape[0]
    e = N_EXPERTS
    per = e // N_GROUPS

    @pl.when(i == 0)
    def _():
        carry_ref[...] = jnp.zeros_like(carry_ref)

    h = _norm_mod(x_ref[...], g_ref, mod_ref, 3, 4)
    h_ref[...] = h.astype(BF16)
    hp_ref[...] = _pack_halves(h)

    h_hi = h.astype(BF16)
    h_lo = (h - h_hi.astype(F32)).astype(BF16)
    wr = wr_ref[...]
    w_hi = wr.astype(BF16)
    w_lo = (wr - w_hi.astype(F32)).astype(BF16)
    logits = _dot_nt(w_hi, h_hi) + (_dot_nt(w_hi, h_lo) + _dot_nt(w_lo, h_hi))
    scores = jax.nn.sigmoid(logits)
    choice = scores + rb_ref[...]

    iota_p = lax.broadcasted_iota(I32, (per, bm), 0)
    gs_rows = []
    for g in range(N_GROUPS):
        cg = choice[g * per:(g + 1) * per, :]
        m1, i1 = _first_argmax(cg, iota_p, per)
        m2 = jnp.max(jnp.where(iota_p == i1, -jnp.inf, cg), axis=0, keepdims=True)
        gs_rows.append(m1 + m2)
    gs = jnp.concatenate(gs_rows, axis=0)

    iota_g = lax.broadcasted_iota(I32, (N_GROUPS, bm), 0)
    gsel = jnp.zeros((N_GROUPS, bm), F32)
    for _ in range(TOPK_GROUPS):
        _, ig = _first_argmax(gs, iota_g, N_GROUPS)
        hit = iota_g == ig
        gsel = jnp.where(hit, 1.0, gsel)
        gs = jnp.where(hit, -jnp.inf, gs)
    emask = jnp.concatenate(
        [jnp.broadcast_to(gsel[g:g + 1, :], (per, bm)) for g in range(N_GROUPS)], axis=0)

    iota_e = lax.broadcasted_iota(I32, (e, bm), 0)
    masked = jnp.where(emask > 0.5, choice, -jnp.inf)
    self_f = jnp.zeros((e, bm), F32)
    idx_rows, w_rows = [], []
    for _ in range(TOP_K):
        _, ie = _first_argmax(masked, iota_e, e)
        hit = iota_e == ie
        self_f = jnp.where(hit, 1.0, self_f)
        idx_rows.append(ie)
        w_rows.append(jnp.sum(jnp.where(hit, scores, 0.0), axis=0, keepdims=True))
        masked = jnp.where(hit, -jnp.inf, masked)
    w = jnp.concatenate(w_rows, axis=0)
    w = w / jnp.sum(w, axis=0, keepdims=True) * ROUTED_SCALE

    t_row = lax.broadcasted_iota(I32, (bm, bm), 0)
    t_col = lax.broadcasted_iota(I32, (bm, bm), 1)
    before = jnp.where(t_row < t_col, 1.0, 0.0).astype(BF16)
    rank_mat = _dot(self_f.astype(BF16), before) + carry_ref[:, :1]
    rank_rows = [jnp.sum(jnp.where(iota_e == ie, rank_mat, 0.0), axis=0, keepdims=True)
                 for ie in idx_rows]
    carry_ref[...] = carry_ref[...] + jnp.sum(self_f, axis=1, keepdims=True)

    pad_i = jnp.zeros((K_PAD - TOP_K, bm), I32)
    ids_ref[...] = jnp.concatenate(idx_rows + [pad_i], axis=0)
    wts_ref[...] = jnp.concatenate([w, jnp.zeros((K_PAD - TOP_K, bm), F32)], axis=0)
    rank_ref[...] = jnp.concatenate([r.astype(I32) for r in rank_rows] + [pad_i], axis=0)
    cnt_ref[...] = carry_ref[...].astype(I32)


def _router(x1, g, mod, w_router, router_bias, seq, bm=512):
    n, d = x1.shape
    e = N_EXPERTS
    tok = lambda i: (0, i)
    return pl.pallas_call(
        _router_kernel,
        grid=(n // bm,),
        in_specs=[
            pl.BlockSpec((bm, d), lambda i: (i, 0)),
            pl.BlockSpec((1, d), lambda i: (0, 0)),
            pl.BlockSpec((1, N_MOD, d), lambda i: (i * bm // seq, 0, 0)),
            pl.BlockSpec((e, d), lambda i: (0, 0)),
            pl.BlockSpec((e, 1), lambda i: (0, 0)),
        ],
        out_specs=[
            pl.BlockSpec((bm, d), lambda i: (i, 0)),
            pl.BlockSpec((bm, d // 2), lambda i: (i, 0)),
            pl.BlockSpec((K_PAD, bm), tok),
            pl.BlockSpec((K_PAD, bm), tok),
            pl.BlockSpec((K_PAD, bm), tok),
            pl.BlockSpec((e, 128), lambda i: (0, 0)),
        ],
        out_shape=[
            jax.ShapeDtypeStruct((n, d), BF16),
            jax.ShapeDtypeStruct((n, d // 2), U32),
            jax.ShapeDtypeStruct((K_PAD, n), I32),
            jax.ShapeDtypeStruct((K_PAD, n), F32),
            jax.ShapeDtypeStruct((K_PAD, n), I32),
            jax.ShapeDtypeStruct((e, 128), I32),
        ],
        scratch_shapes=[pltpu.VMEM((e, 128), F32)],
        compiler_params=_cparams(("arbitrary",)),
        name="router",
    )(x1, g.reshape(1, d), mod, w_router.T, router_bias.reshape(e, 1))


def _pos_kernel(off_ref, ids_ref, rank_ref, o_ref):
    ids = ids_ref[...]
    pos = rank_ref[...]
    for ex in range(N_EXPERTS):
        pos = pos + jnp.where(ids == ex, off_ref[ex], 0)
    o_ref[...] = pos


def _positions(offsets, ids, ranks):
    return pl.pallas_call(
        _pos_kernel,
        in_specs=[
            pl.BlockSpec(memory_space=pltpu.SMEM),
            pl.BlockSpec(memory_space=pltpu.VMEM),
            pl.BlockSpec(memory_space=pltpu.VMEM),
        ],
        out_specs=pl.BlockSpec(memory_space=pltpu.VMEM),
        out_shape=jax.ShapeDtypeStruct(ids.shape, I32),
        name="positions",
    )(offsets, ids, ranks)


def _dispatch_kernel(pad_ref, pos_ref, h_ref, xs_ref, zeros, sem, pad_sem, *, tm):
    bt = h_ref.shape[0]

    @pl.when(pl.program_id(0) == 0)
    def _():
        zeros[...] = jnp.zeros_like(zeros)
        group_bits = [1 << b for b in reversed(range((tm // SUBLANES - 1).bit_length()))]

        def pad_copies(ex):
            start, gap = pad_ref[0, ex], pad_ref[1, ex]
            end = start + gap
            groups, singles = gap >> 3, gap & (SUBLANES - 1)
            out = []
            for j in range(SUBLANES - 1):
                out.append((j < singles, pltpu.make_async_copy(
                    zeros.at[pl.ds(0, 1)], xs_ref.at[pl.ds(start + j, 1)], pad_sem)))
            for bit in group_bits:
                first = pl.multiple_of(end - SUBLANES * (groups & (2 * bit - 1)), SUBLANES)
                out.append(((groups & bit) != 0, pltpu.make_async_copy(
                    zeros.at[pl.ds(0, SUBLANES * bit)], xs_ref.at[pl.ds(first, SUBLANES * bit)], pad_sem)))
            return out

        def issue_pad(ex, _):
            for cond, cp in pad_copies(ex):
                @pl.when(cond)
                def _():
                    cp.start()
            return 0

        def drain_pad(ex, _):
            for cond, cp in pad_copies(ex):
                @pl.when(cond)
                def _():
                    cp.wait()
            return 0

        lax.fori_loop(0, N_EXPERTS, issue_pad, 0)
        lax.fori_loop(0, N_EXPERTS, drain_pad, 0)

        zrows = zeros.shape[0]
        first = (pad_ref[0, N_EXPERTS - 1] + pad_ref[1, N_EXPERTS - 1]) // zrows

        def tail_copy(g):
            return pltpu.make_async_copy(zeros, xs_ref.at[pl.ds(pl.multiple_of(g * zrows, zrows), zrows)],
                                         pad_sem)

        def issue_tail(g, _):
            tail_copy(g).start()
            return 0

        def drain_tail(g, _):
            tail_copy(g).wait()
            return 0

        lax.fori_loop(first, xs_ref.shape[0] // zrows, issue_tail, 0)
        lax.fori_loop(first, xs_ref.shape[0] // zrows, drain_tail, 0)

    def issue(r, _):
        for k in range(TOP_K):
            p = pos_ref[0, k, r]
            pltpu.make_async_copy(h_ref.at[pl.ds(r, 1)], xs_ref.at[pl.ds(p, 1)], sem).start()
        return 0

    lax.fori_loop(0, bt, issue, 0, unroll=ROW_DMA_UNROLL)
    for _ in range(TOP_K):
        pltpu.make_async_copy(h_ref, xs_ref.at[pl.ds(0, bt)], sem).wait()


def _dispatch(hp, pos3, pads, n_rows, bt, tm):
    n, c = hp.shape
    return pl.pallas_call(
        functools.partial(_dispatch_kernel, tm=tm),
        grid=(n // bt,),
        in_specs=[
            pl.BlockSpec(memory_space=pltpu.SMEM),
            pl.BlockSpec((1, K_PAD, bt), lambda i: (i, 0, 0), memory_space=pltpu.SMEM),
            pl.BlockSpec((bt, c), lambda i: (i, 0)),
        ],
        out_specs=pl.BlockSpec(memory_space=pl.ANY),
        out_shape=jax.ShapeDtypeStruct((n_rows, c), U32),
        scratch_shapes=[pltpu.VMEM((tm // 2, c), U32), pltpu.SemaphoreType.DMA(()),
                        pltpu.SemaphoreType.DMA(())],
        compiler_params=_cparams(("arbitrary",)),
        name="dispatch",
    )(pads, pos3, hp)


def _expert_tiles(ts_ref, src_ref, dst_ref, in_buf, out_buf, zero_buf, in_sem, out_sem, zero_sem, compute):
    e = pl.program_id(0)
    n_exp = pl.num_programs(0)
    tm = in_buf.shape[1]
    n_tiles = dst_ref.shape[0] // tm
    n_used = ts_ref[n_exp]

    def rows(g):
        return pl.ds(pl.multiple_of(g * tm, tm), tm)

    def load(g):
        return pltpu.make_async_copy(src_ref.at[rows(g)], in_buf.at[g % 2], in_sem.at[g % 2])

    def store(g):
        return pltpu.make_async_copy(out_buf.at[g % 2], dst_ref.at[rows(g)], out_sem.at[g % 2])

    @pl.when(e == 0)
    def _():
        load(0).start()

    def tile(g, _):
        @pl.when(g + 1 < n_used)
        def _():
            load(g + 1).start()

        load(g).wait()

        @pl.when(g >= 2)
        def _():
            store(g - 2).wait()

        out_buf[g % 2] = compute(in_buf[g % 2])
        store(g).start()
        return 0

    lax.fori_loop(ts_ref[e], ts_ref[e + 1], tile, 0)

    @pl.when(e == n_exp - 1)
    def _():
        @pl.when(n_used >= 2)
        def _():
            store(n_used - 2).wait()

        store(n_used - 1).wait()

        zero_buf[...] = jnp.zeros_like(zero_buf)

        def fill(g):
            return pltpu.make_async_copy(zero_buf, dst_ref.at[rows(g)], zero_sem)

        def start_fill(g, _):
            fill(g).start()
            return 0

        def wait_fill(g, _):
            fill(g).wait()
            return 0

        lax.fori_loop(n_used, n_tiles, start_fill, 0)
        lax.fori_loop(n_used, n_tiles, wait_fill, 0)


def _expert_up_kernel(ts_ref, wg_ref, wu_ref, xs_ref, act_ref, w_scr, in_buf, out_buf, zero_buf,
                      in_sem, out_sem, zero_sem):
    e = pl.program_id(0)
    f = wg_ref.shape[2]

    @pl.when(ts_ref[e + 1] > ts_ref[e])
    def _():
        w_scr[:, :f] = wg_ref[0].astype(BF16)
        w_scr[:, f:] = wu_ref[0].astype(BF16)

    def compute(xp):
        lo, hi = _unpack_halves(xp)
        x = jnp.concatenate([lo.astype(BF16), hi.astype(BF16)], axis=1)
        gu = _dot(x, w_scr[...])
        g, u = gu[:, :f], gu[:, f:]
        return (g * jax.nn.sigmoid(g) * u).astype(BF16)

    _expert_tiles(ts_ref, xs_ref, act_ref, in_buf, out_buf, zero_buf, in_sem, out_sem, zero_sem, compute)


def _expert_up(tile_start, xs, w_gate_e, w_up_e, tm):
    p, c = xs.shape
    e, d, f = w_gate_e.shape
    grid_spec = pltpu.PrefetchScalarGridSpec(
        num_scalar_prefetch=1,
        grid=(e,),
        in_specs=[
            pl.BlockSpec((1, d, f), lambda i, ts: (i, 0, 0)),
            pl.BlockSpec((1, d, f), lambda i, ts: (i, 0, 0)),
            pl.BlockSpec(memory_space=pl.ANY),
        ],
        out_specs=pl.BlockSpec(memory_space=pl.ANY),
        scratch_shapes=[
            pltpu.VMEM((d, 2 * f), BF16),
            pltpu.VMEM((2, tm, c), U32),
            pltpu.VMEM((2, tm, f), BF16),
            pltpu.VMEM((tm, f), BF16),
            pltpu.SemaphoreType.DMA((2,)),
            pltpu.SemaphoreType.DMA((2,)),
            pltpu.SemaphoreType.DMA(()),
        ],
    )
    return pl.pallas_call(
        _expert_up_kernel,
        grid_spec=grid_spec,
        out_shape=jax.ShapeDtypeStruct((p, f), BF16),
        compiler_params=_cparams(("arbitrary",)),
        name="expert_up",
    )(tile_start, w_gate_e, w_up_e, xs)


def _expert_down_kernel(ts_ref, wd_ref, act_ref, ys_ref, w_scr, in_buf, out_buf, zero_buf,
                        in_sem, out_sem, zero_sem):
    e = pl.program_id(0)

    @pl.when(ts_ref[e + 1] > ts_ref[e])
    def _():
        w_scr[...] = wd_ref[0].astype(BF16)

    def compute(a):
        return _pack_halves(_dot(a, w_scr[...]))

    _expert_tiles(ts_ref, act_ref, ys_ref, in_buf, out_buf, zero_buf, in_sem, out_sem, zero_sem, compute)


def _expert_down(tile_start, act, w_down_e, tm):
    p, f = act.shape
    e, _, d = w_down_e.shape
    grid_spec = pltpu.PrefetchScalarGridSpec(
        num_scalar_prefetch=1,
        grid=(e,),
        in_specs=[
            pl.BlockSpec((1, f, d), lambda i, ts: (i, 0, 0)),
            pl.BlockSpec(memory_space=pl.ANY),
        ],
        out_specs=pl.BlockSpec(memory_space=pl.ANY),
        scratch_shapes=[
            pltpu.VMEM((f, d), BF16),
            pltpu.VMEM((2, tm, f), BF16),
            pltpu.VMEM((2, tm, d // 2), U32),
            pltpu.VMEM((tm, d // 2), U32),
            pltpu.SemaphoreType.DMA((2,)),
            pltpu.SemaphoreType.DMA((2,)),
            pltpu.SemaphoreType.DMA(()),
        ],
    )
    return pl.pallas_call(
        _expert_down_kernel,
        grid_spec=grid_spec,
        out_shape=jax.ShapeDtypeStruct((p, d // 2), U32),
        compiler_params=_cparams(("arbitrary",)),
        name="expert_down",
    )(tile_start, w_down_e, act)


def _shared_up_kernel(h_ref, wg_ref, wu_ref, o_ref):
    h = h_ref[...]
    g = _dot(h, wg_ref[...].astype(BF16))
    u = _dot(h, wu_ref[...].astype(BF16))
    o_ref[...] = (g * jax.nn.sigmoid(g) * u).astype(o_ref.dtype)


def _shared_up(h, w_gate_s, w_up_s, bm=1024, bn=256):
    n, d = h.shape
    f = w_gate_s.shape[1]
    return pl.pallas_call(
        _shared_up_kernel,
        grid=(n // bm, f // bn),
        in_specs=[
            pl.BlockSpec((bm, d), lambda i, j: (i, 0)),
            pl.BlockSpec((d, bn), lambda i, j: (0, j)),
            pl.BlockSpec((d, bn), lambda i, j: (0, j)),
        ],
        out_specs=pl.BlockSpec((bm, bn), lambda i, j: (i, j)),
        out_shape=jax.ShapeDtypeStruct((n, f), BF16),
        compiler_params=_cparams(("arbitrary", "arbitrary")),
        name="shared_up",
    )(h, w_gate_s, w_up_s)


def _combine_kernel(pos_ref, pos_next_ref, x_ref, s_ref, wt_ref, mod_ref, g_ref, ys_ref, wd_ref, o_ref,
                    ybuf, wd_scr, stage, sem, wd_sem):
    i = pl.program_id(0)
    bt = x_ref.shape[0]
    slot = i % 2

    @pl.when(i == 0)
    def _():
        rows = stage.shape[0]
        for part in range(wd_scr.shape[0] // rows):
            cp = pltpu.make_async_copy(wd_ref.at[pl.ds(part * rows, rows)], stage, wd_sem)
            cp.start()
            cp.wait()
            wd_scr[part * rows:(part + 1) * rows, :] = stage[...].astype(BF16)

    def gather(p_ref, dst_slot):
        def issue(r, _):
            for k in range(TOP_K):
                p = p_ref[0, k, r]
                pltpu.make_async_copy(ys_ref.at[pl.ds(p, 1)], ybuf.at[dst_slot, k, pl.ds(r, 1)],
                                      sem.at[dst_slot]).start()
            return 0

        lax.fori_loop(0, bt, issue, 0, unroll=ROW_DMA_UNROLL)

    @pl.when(i == 0)
    def _():
        gather(pos_ref, 0)

    @pl.when(i + 1 < pl.num_programs(0))
    def _():
        gather(pos_next_ref, 1 - slot)

    for k in range(TOP_K):
        pltpu.make_async_copy(ys_ref.at[pl.ds(0, bt)], ybuf.at[slot, k], sem.at[slot]).wait()

    c = ybuf.shape[3]
    acc_lo = jnp.zeros((bt, c), F32)
    acc_hi = jnp.zeros((bt, c), F32)
    for k in range(TOP_K):
        lo, hi = _unpack_halves(ybuf[slot, k])
        wk = wt_ref[:, k:k + 1]
        acc_lo = acc_lo + wk * lo
        acc_hi = acc_hi + wk * hi
    moe = jnp.concatenate([acc_lo, acc_hi], axis=1) + _dot(s_ref[...], wd_scr[...])
    gate = mod_ref[0][5:6]
    x = x_ref[...] + gate * moe
    y = x * lax.rsqrt(jnp.mean(x * x, axis=-1, keepdims=True) + EPS) * g_ref[...]
    o_ref[...] = y


def _combine(pos3, x1, sact, wts_t, mod, final_g, ys, w_down_s, seq, bt, stage_rows=128):
    n, d = x1.shape
    f = sact.shape[1]
    nb = n // bt
    pos_spec = lambda imap: pl.BlockSpec((1, K_PAD, bt), imap, memory_space=pltpu.SMEM)
    return pl.pallas_call(
        _combine_kernel,
        grid=(nb,),
        in_specs=[
            pos_spec(lambda i: (i, 0, 0)),
            pos_spec(lambda i: (jnp.minimum(i + 1, nb - 1), 0, 0)),
            pl.BlockSpec((bt, d), lambda i: (i, 0)),
            pl.BlockSpec((bt, f), lambda i: (i, 0)),
            pl.BlockSpec((bt, K_PAD), lambda i: (i, 0)),
            pl.BlockSpec((1, N_MOD, d), lambda i: (i * bt // seq, 0, 0)),
            pl.BlockSpec((1, d), lambda i: (0, 0)),
            pl.BlockSpec(memory_space=pl.ANY),
            pl.BlockSpec(memory_space=pl.ANY),
        ],
        out_specs=pl.BlockSpec((bt, d), lambda i: (i, 0)),
        out_shape=jax.ShapeDtypeStruct((n, d), F32),
        scratch_shapes=[
            pltpu.VMEM((2, TOP_K, bt, d // 2), U32),
            pltpu.VMEM((f, d), BF16),
            pltpu.VMEM((stage_rows, d), F32),
            pltpu.SemaphoreType.DMA((2,)),
            pltpu.SemaphoreType.DMA(()),
        ],
        compiler_params=_cparams(("arbitrary",)),
        name="combine",
    )(pos3, pos3, x1, sact, wts_t, mod, final_g.reshape(1, d), ys, w_down_s)


def _moe_tiles(counts, n_tokens, tm):
    e = counts.shape[0]
    n_tiles = n_tokens * TOP_K // tm + e
    tiles_per = (counts + tm - 1) // tm
    tile_end = jnp.cumsum(tiles_per)
    tile_start = jnp.concatenate([jnp.zeros((1,), I32), tile_end.astype(I32)])
    offsets = tile_start[:-1] * tm
    pads = jnp.stack([offsets + counts, tiles_per * tm - counts]).astype(I32)
    return offsets, tile_start, pads, n_tiles


def kernel(x, c, w_ada, b_ada, norm1_g, w_in, w_s, b_s, w_up_a, w_up_b, w_o, norm2_g, w_router,
           router_bias, w_gate_e, w_up_e, w_down_e, w_gate_s, w_up_s, w_down_s, final_g):
    batch, seq, d = x.shape
    n = batch * seq
    depth = w_ada.shape[0]
    a_width = w_up_a.shape[1]
    b_width = w_up_b.shape[1]
    n_heads = b_width // HEAD_DIM
    tm = 256
    bt_dispatch = 256
    bt_combine = 128

    assert depth == 1, "the combine kernel applies the final norm, so only one layer is supported"
    l = 0
    x2 = x.reshape(n, d)
    mod = _ada(c, w_ada[l], b_ada[l])
    h1 = _norm1(x2, norm1_g[l], mod, seq)
    proj = _inproj(h1, w_in[l], a_width, b_width)
    sgu = _sgu(proj, w_s[l], b_s[l], a_width)
    att = _attention(proj, batch, seq, 2 * a_width, 2 * a_width + b_width,
                     2 * a_width + 2 * b_width, n_heads)
    merged = _merge(sgu, att, w_up_a[l], w_up_b[l], proj,
                    2 * a_width + 3 * b_width, 2 * a_width + 3 * b_width + d)
    x1 = _resid_matmul(merged, w_o[l], x2, mod, 2, seq, "out_proj")

    h2, h2p, ids, wts, ranks, counts = _router(x1, norm2_g[l], mod, w_router[l], router_bias[l], seq)
    offsets, tile_start, pads, n_tiles = _moe_tiles(counts[:, 0], n, tm)
    pos = _positions(offsets, ids, ranks)
    blocked = lambda b: pos.reshape(K_PAD, n // b, b).transpose(1, 0, 2)
    xs = _dispatch(h2p, blocked(bt_dispatch), pads, n_tiles * tm, bt_dispatch, tm)
    act = _expert_up(tile_start, xs, w_gate_e[l], w_up_e[l], tm)
    ys = _expert_down(tile_start, act, w_down_e[l], tm)
    sact = _shared_up(h2, w_gate_s[l], w_up_s[l])
    out = _combine(blocked(bt_combine), x1, sact, wts.T, mod, final_g, ys, w_down_s[l], seq, bt_combine)
    return out.reshape(batch, seq, d)
```

```python
import functools

import jax
import jax.numpy as jnp
from jax import lax
from jax.experimental import pallas as pl
from jax.experimental.pallas import tpu as pltpu

F32 = jnp.float32
BF16 = jnp.bfloat16
I32 = jnp.int32
U32 = jnp.uint32

EPS = 1e-6
A_GROUPS = 8
A_CHUNK = 128
HEAD_DIM = 128
N_EXPERTS = 64
TOP_K = 6
N_GROUPS = 8
TOPK_GROUPS = 4
ROUTED_SCALE = 2.5
N_MOD = 6
SUBLANES = 8
K_PAD = SUBLANES
ROW_DMA_UNROLL = 8
DMA_QUEUES = 2
TILE_DMA_PRIORITY = 1
DEAD_LOG = -105.0

VMEM_LIMIT = 56 * 1024 * 1024


def _cparams(sem):
    return pltpu.CompilerParams(dimension_semantics=sem, vmem_limit_bytes=VMEM_LIMIT)


def _dot(a, b):
    return jnp.dot(a, b, preferred_element_type=F32)


def _dot_nt(a, b):
    return lax.dot_general(a, b, (((1,), (1,)), ((), ())), preferred_element_type=F32)


def _pack_halves(h):
    c = h.shape[1] // 2
    lo = pltpu.bitcast(h[:, :c].astype(BF16).astype(F32), U32)
    hi = pltpu.bitcast(h[:, c:].astype(BF16).astype(F32), U32)
    return (hi & jnp.uint32(0xFFFF0000)) | (lo >> 16)


def _unpack_halves(p):
    lo = pltpu.bitcast(p << 16, F32)
    hi = pltpu.bitcast(p & jnp.uint32(0xFFFF0000), F32)
    return lo, hi


def _ada_kernel(c_ref, w_ref, b_ref, o_ref):
    c = c_ref[...]
    cact = (c * jax.nn.sigmoid(c)).astype(BF16)
    o_ref[...] = _dot(cact, w_ref[...].astype(BF16)) + b_ref[...]


def _ada(c, w_ada, b_ada, bn=512):
    b, d = c.shape
    n = w_ada.shape[1]
    cp = jnp.zeros((8, d), F32).at[:b].set(c)
    out = pl.pallas_call(
        _ada_kernel,
        grid=(n // bn,),
        in_specs=[
            pl.BlockSpec((8, d), lambda j: (0, 0)),
            pl.BlockSpec((d, bn), lambda j: (0, j)),
            pl.BlockSpec((1, bn), lambda j: (0, j)),
        ],
        out_specs=pl.BlockSpec((8, bn), lambda j: (0, j)),
        out_shape=jax.ShapeDtypeStruct((8, n), F32),
        compiler_params=_cparams(("arbitrary",)),
        name="ada",
    )(cp, w_ada, b_ada.reshape(1, n))
    return out[:b].reshape(b, N_MOD, d)


def _norm_mod(x, g_ref, mod_ref, shift_idx, scale_idx):
    y = x * lax.rsqrt(jnp.mean(x * x, axis=-1, keepdims=True) + EPS) * g_ref[...]
    mod = mod_ref[0]
    return y * (1.0 + mod[scale_idx:scale_idx + 1]) + mod[shift_idx:shift_idx + 1]


def _norm1_kernel(x_ref, g_ref, mod_ref, o_ref):
    o_ref[...] = _norm_mod(x_ref[...], g_ref, mod_ref, 0, 1).astype(BF16)


def _norm1(x2, g, mod, seq, bm=512):
    n, d = x2.shape
    return pl.pallas_call(
        _norm1_kernel,
        grid=(n // bm,),
        in_specs=[
            pl.BlockSpec((bm, d), lambda i: (i, 0)),
            pl.BlockSpec((1, d), lambda i: (0, 0)),
            pl.BlockSpec((1, N_MOD, d), lambda i: (i * bm // seq, 0, 0)),
        ],
        out_specs=pl.BlockSpec((bm, d), lambda i: (i, 0)),
        out_shape=jax.ShapeDtypeStruct((n, d), BF16),
        compiler_params=_cparams(("arbitrary",)),
        name="norm1",
    )(x2, g.reshape(1, d), mod)


def _inproj_kernel(h_ref, w_ref, o_ref, *, n_gelu, n_ln, n_plain, group):
    j = pl.program_id(1)

    def run(epilogue):
        for s in range(o_ref.shape[1] // group):
            cs = slice(s * group, (s + 1) * group)
            acc = _dot(h_ref[...], w_ref[:, cs].astype(BF16))
            o_ref[:, cs] = epilogue(acc).astype(o_ref.dtype)

    def group_norm(acc):
        a = jax.nn.gelu(acc)
        cen = a - jnp.mean(a, axis=-1, keepdims=True)
        var = jnp.mean(cen * cen, axis=-1, keepdims=True)
        return cen * lax.rsqrt(var + EPS)

    @pl.when(j < n_gelu)
    def _():
        run(jax.nn.gelu)

    @pl.when((j >= n_gelu) & (j < n_ln))
    def _():
        run(group_norm)

    @pl.when((j >= n_ln) & (j < n_plain))
    def _():
        run(lambda acc: acc)

    @pl.when(j >= n_plain)
    def _():
        run(jax.nn.sigmoid)


def _inproj(h, w_in, a_width, b_width, bm=1024, bn=512):
    n, d = h.shape
    cols = w_in.shape[1]
    group = a_width // A_GROUPS
    kern = functools.partial(
        _inproj_kernel,
        n_gelu=a_width // bn,
        n_ln=2 * a_width // bn,
        n_plain=(2 * a_width + 3 * b_width) // bn,
        group=group,
    )
    return pl.pallas_call(
        kern,
        grid=(n // bm, cols // bn),
        in_specs=[
            pl.BlockSpec((bm, d), lambda i, j: (i, 0)),
            pl.BlockSpec((d, bn), lambda i, j: (0, j)),
        ],
        out_specs=pl.BlockSpec((bm, bn), lambda i, j: (i, j)),
        out_shape=jax.ShapeDtypeStruct((n, cols), BF16),
        compiler_params=_cparams(("arbitrary", "arbitrary")),
        name="inproj",
    )(h, w_in)


def _sgu_kernel(u_ref, v_ref, ws_ref, bs_ref, o_ref, *, group):
    t = A_CHUNK
    row = lax.broadcasted_iota(I32, (t, t), 0)
    col = lax.broadcasted_iota(I32, (t, t), 1)
    for g in range(A_GROUPS):
        ws = jnp.where(row >= col, ws_ref[g], 0.0).astype(BF16)
        bias = bs_ref[:, g:g + 1]
        for c in range(u_ref.shape[0] // t):
            rs = slice(c * t, (c + 1) * t)
            cs = slice(g * group, (g + 1) * group)
            mixed = _dot(ws, v_ref[rs, cs]) + bias
            o_ref[rs, cs] = (u_ref[rs, cs].astype(F32) * mixed).astype(o_ref.dtype)


def _sgu(proj, w_s, b_s, a_width, bm=512):
    n = proj.shape[0]
    group = a_width // A_GROUPS
    return pl.pallas_call(
        functools.partial(_sgu_kernel, group=group),
        grid=(n // bm,),
        in_specs=[
            pl.BlockSpec((bm, a_width), lambda i: (i, 0)),
            pl.BlockSpec((bm, a_width), lambda i: (i, 1)),
            pl.BlockSpec((A_GROUPS, A_CHUNK, A_CHUNK), lambda i: (0, 0, 0)),
            pl.BlockSpec((A_CHUNK, A_GROUPS), lambda i: (0, 0)),
        ],
        out_specs=pl.BlockSpec((bm, a_width), lambda i: (i, 0)),
        out_shape=jax.ShapeDtypeStruct((n, a_width), BF16),
        compiler_params=_cparams(("arbitrary",)),
        name="sgu",
    )(proj, proj, w_s, b_s.T)


def _attn_kernel(q_ref, k_ref, v_ref, o_ref, *, blk, heads, scale):
    seq = q_ref.shape[0]
    row = lax.broadcasted_iota(I32, (blk, blk), 0)
    col = lax.broadcasted_iota(I32, (blk, blk), 1)
    upper = jnp.where(row > col, 1.0, 0.0).astype(BF16)
    strict = col < row

    hslices = [slice(h * HEAD_DIM, (h + 1) * HEAD_DIM) for h in range(heads)]

    def update(qv, ks, cs, accs, diagonal):
        zs = [_dot_nt(qv[h], k_ref[pl.ds(ks, blk), hslices[h]]) * scale for h in range(heads)]
        sps = [jnp.maximum(z, 0.0) + jnp.log(1.0 + jnp.exp(-jnp.abs(z))) for z in zs]
        stays = [jnp.where(strict, -sp, 0.0) if diagonal else -sp for sp in sps]
        his = [s.astype(BF16) for s in stays]
        los = [(s - hi.astype(F32)).astype(BF16) for s, hi in zip(stays, his)]
        laters = [_dot(hi, upper) + _dot(lo, upper) for hi, lo in zip(his, los)]
        ws = [jnp.exp((zs[h] - sps[h]) + laters[h] + cs[h]) for h in range(heads)]
        if diagonal:
            ws = [jnp.where(strict, w, 0.0) for w in ws]
        accs = tuple(accs[h] + _dot(ws[h].astype(BF16), v_ref[pl.ds(ks, blk), hslices[h]])
                     for h in range(heads))
        cs = tuple(cs[h] + laters[h][:, :1] + stays[h][:, :1] for h in range(heads))
        return cs, accs

    def q_block(qi, _):
        qs = pl.multiple_of(qi * blk, blk)
        qv = [q_ref[pl.ds(qs, blk), hs] for hs in hslices]
        cs, accs = update(qv, qs, (jnp.zeros((blk, 1), F32),) * heads,
                          (jnp.zeros((blk, HEAD_DIM), F32),) * heads, True)

        def live(carry):
            n, cs, _ = carry
            cmax = jnp.max(cs[0])
            for c in cs[1:]:
                cmax = jnp.maximum(cmax, jnp.max(c))
            return (n <= qi) & (cmax > DEAD_LOG)

        def older_block(carry):
            n, cs, accs = carry
            ks = pl.multiple_of((qi - n) * blk, blk)
            cs, accs = update(qv, ks, cs, accs, False)
            return n + 1, cs, accs

        _, _, accs = lax.while_loop(live, older_block, (jnp.int32(1), cs, accs))
        for h in range(heads):
            o_ref[pl.ds(qs, blk), hslices[h]] = accs[h].astype(o_ref.dtype)
        return 0

    lax.fori_loop(0, seq // blk, q_block, 0)


def _attention(proj, batch, seq, q_col, k_col, v_col, n_heads, blk=256, heads=4):
    width = heads * HEAD_DIM
    qc, kc, vc = q_col // width, k_col // width, v_col // width
    return pl.pallas_call(
        functools.partial(_attn_kernel, blk=blk, heads=heads, scale=HEAD_DIM ** -0.5),
        grid=(batch, n_heads // heads),
        in_specs=[
            pl.BlockSpec((seq, width), lambda b, h: (b, qc + h)),
            pl.BlockSpec((seq, width), lambda b, h: (b, kc + h)),
            pl.BlockSpec((seq, width), lambda b, h: (b, vc + h)),
        ],
        out_specs=pl.BlockSpec((seq, width), lambda b, h: (b, h)),
        out_shape=jax.ShapeDtypeStruct((batch * seq, n_heads * HEAD_DIM), BF16),
        compiler_params=_cparams(("arbitrary", "arbitrary")),
        name="attn",
    )(proj, proj, proj)


def _merge_kernel(a_ref, b_ref, wa_ref, wb_ref, ga_ref, gb_ref, o_ref):
    ya = _dot(a_ref[...], wa_ref[...].astype(BF16))
    yb = _dot(b_ref[...], wb_ref[...].astype(BF16))
    o_ref[...] = (ga_ref[...].astype(F32) * ya + gb_ref[...].astype(F32) * yb).astype(o_ref.dtype)


def _merge(sgu, att, w_up_a, w_up_b, proj, ga_col, gb_col, bm=1024, bn=512):
    n, ka = sgu.shape
    kb = att.shape[1]
    d = w_up_a.shape[1]
    gac, gbc = ga_col // bn, gb_col // bn
    return pl.pallas_call(
        _merge_kernel,
        grid=(n // bm, d // bn),
        in_specs=[
            pl.BlockSpec((bm, ka), lambda i, j: (i, 0)),
            pl.BlockSpec((bm, kb), lambda i, j: (i, 0)),
            pl.BlockSpec((ka, bn), lambda i, j: (0, j)),
            pl.BlockSpec((kb, bn), lambda i, j: (0, j)),
            pl.BlockSpec((bm, bn), lambda i, j: (i, gac + j)),
            pl.BlockSpec((bm, bn), lambda i, j: (i, gbc + j)),
        ],
        out_specs=pl.BlockSpec((bm, bn), lambda i, j: (i, j)),
        out_shape=jax.ShapeDtypeStruct((n, d), BF16),
        compiler_params=_cparams(("arbitrary", "arbitrary")),
        name="merge",
    )(sgu, att, w_up_a, w_up_b, proj, proj)


def _resid_kernel(a_ref, w_ref, x_ref, mod_ref, o_ref, *, gate_idx):
    y = _dot(a_ref[...], w_ref[...].astype(BF16))
    gate = mod_ref[0][gate_idx:gate_idx + 1]
    o_ref[...] = x_ref[...] + gate * y


def _resid_matmul(a, w, x2, mod, gate_idx, seq, name, bm=1024, bn=512):
    n, k = a.shape
    d = w.shape[1]
    return pl.pallas_call(
        functools.partial(_resid_kernel, gate_idx=gate_idx),
        grid=(n // bm, d // bn),
        in_specs=[
            pl.BlockSpec((bm, k), lambda i, j: (i, 0)),
            pl.BlockSpec((k, bn), lambda i, j: (0, j)),
            pl.BlockSpec((bm, bn), lambda i, j: (i, j)),
            pl.BlockSpec((1, N_MOD, bn), lambda i, j: (i * bm // seq, 0, j)),
        ],
        out_specs=pl.BlockSpec((bm, bn), lambda i, j: (i, j)),
        out_shape=jax.ShapeDtypeStruct((n, d), F32),
        compiler_params=_cparams(("arbitrary", "arbitrary")),
        name=name,
    )(a, w, x2, mod)


def _first_argmax(vals, iota, size):
    m = jnp.max(vals, axis=0, keepdims=True)
    idx = jnp.min(jnp.where(vals == m, iota, size), axis=0, keepdims=True)
    return m, idx


def _router_kernel(x_ref, g_ref, mod_ref, wr_ref, rb_ref, h_ref, hp_ref, ids_ref, wts_ref, rank_ref,
                   cnt_ref, carry_ref):
    i = pl.program_id(0)
    bm = x_ref.shape[0]
    e = N_EXPERTS
    per = e // N_GROUPS

    @pl.when(i == 0)
    def _():
        carry_ref[...] = jnp.zeros_like(carry_ref)

    h = _norm_mod(x_ref[...], g_ref, mod_ref, 3, 4)
    h_ref[...] = h.astype(BF16)
    hp_ref[...] = _pack_halves(h)

    h_hi = h.astype(BF16)
    h_lo = (h - h_hi.astype(F32)).astype(BF16)
    wr = wr_ref[...]
    w_hi = wr.astype(BF16)
    w_lo = (wr - w_hi.astype(F32)).astype(BF16)
    logits = _dot_nt(w_hi, h_hi) + (_dot_nt(w_hi, h_lo) + _dot_nt(w_lo, h_hi))
    scores = jax.nn.sigmoid(logits)
    choice = scores + rb_ref[...]

    iota_p = lax.broadcasted_iota(I32, (per, bm), 0)
    gs_rows = []
    for g in range(N_GROUPS):
        cg = choice[g * per:(g + 1) * per, :]
        m1, i1 = _first_argmax(cg, iota_p, per)
        m2 = jnp.max(jnp.where(iota_p == i1, -jnp.inf, cg), axis=0, keepdims=True)
        gs_rows.append(m1 + m2)
    gs = jnp.concatenate(gs_rows, axis=0)

    iota_g = lax.broadcasted_iota(I32, (N_GROUPS, bm), 0)
    gsel = jnp.zeros((N_GROUPS, bm), F32)
    for _ in range(TOPK_GROUPS):
        _, ig = _first_argmax(gs, iota_g, N_GROUPS)
        hit = iota_g == ig
        gsel = jnp.where(hit, 1.0, gsel)
        gs = jnp.where(hit, -jnp.inf, gs)
    emask = jnp.concatenate(
        [jnp.broadcast_to(gsel[g:g + 1, :], (per, bm)) for g in range(N_GROUPS)], axis=0)

    iota_e = lax.broadcasted_iota(I32, (e, bm), 0)
    masked = jnp.where(emask > 0.5, choice, -jnp.inf)
    self_f = jnp.zeros((e, bm), F32)
    idx_rows, w_rows = [], []
    for _ in range(TOP_K):
        _, ie = _first_argmax(masked, iota_e, e)
        hit = iota_e == ie
        self_f = jnp.where(hit, 1.0, self_f)
        idx_rows.append(ie)
        w_rows.append(jnp.sum(jnp.where(hit, scores, 0.0), axis=0, keepdims=True))
        masked = jnp.where(hit, -jnp.inf, masked)
    w = jnp.concatenate(w_rows, axis=0)
    w = w / jnp.sum(w, axis=0, keepdims=True) * ROUTED_SCALE

    t_row = lax.broadcasted_iota(I32, (bm, bm), 0)
    t_col = lax.broadcasted_iota(I32, (bm, bm), 1)
    before = jnp.where(t_row < t_col, 1.0, 0.0).astype(BF16)
    rank_mat = _dot(self_f.astype(BF16), before) + carry_ref[:, :1]
    rank_rows = [jnp.sum(jnp.where(iota_e == ie, rank_mat, 0.0), axis=0, keepdims=True)
                 for ie in idx_rows]
    carry_ref[...] = carry_ref[...] + jnp.sum(self_f, axis=1, keepdims=True)

    pad_i = jnp.zeros((K_PAD - TOP_K, bm), I32)
    ids_ref[...] = jnp.concatenate(idx_rows + [pad_i], axis=0)
    wts_ref[...] = jnp.concatenate([w, jnp.zeros((K_PAD - TOP_K, bm), F32)], axis=0)
    rank_ref[...] = jnp.concatenate([r.astype(I32) for r in rank_rows] + [pad_i], axis=0)
    cnt_ref[...] = carry_ref[...].astype(I32)


def _router(x1, g, mod, w_router, router_bias, seq, bm=512):
    n, d = x1.shape
    e = N_EXPERTS
    tok = lambda i: (0, i)
    return pl.pallas_call(
        _router_kernel,
        grid=(n // bm,),
        in_specs=[
            pl.BlockSpec((bm, d), lambda i: (i, 0)),
            pl.BlockSpec((1, d), lambda i: (0, 0)),
            pl.BlockSpec((1, N_MOD, d), lambda i: (i * bm // seq, 0, 0)),
            pl.BlockSpec((e, d), lambda i: (0, 0)),
            pl.BlockSpec((e, 1), lambda i: (0, 0)),
        ],
        out_specs=[
            pl.BlockSpec((bm, d), lambda i: (i, 0)),
            pl.BlockSpec((bm, d // 2), lambda i: (i, 0)),
            pl.BlockSpec((K_PAD, bm), tok),
            pl.BlockSpec((K_PAD, bm), tok),
            pl.BlockSpec((K_PAD, bm), tok),
            pl.BlockSpec((e, 128), lambda i: (0, 0)),
        ],
        out_shape=[
            jax.ShapeDtypeStruct((n, d), BF16),
            jax.ShapeDtypeStruct((n, d // 2), U32),
            jax.ShapeDtypeStruct((K_PAD, n), I32),
            jax.ShapeDtypeStruct((K_PAD, n), F32),
            jax.ShapeDtypeStruct((K_PAD, n), I32),
            jax.ShapeDtypeStruct((e, 128), I32),
        ],
        scratch_shapes=[pltpu.VMEM((e, 128), F32)],
        compiler_params=_cparams(("arbitrary",)),
        name="router",
    )(x1, g.reshape(1, d), mod, w_router.T, router_bias.reshape(e, 1))


def _pos_kernel(off_ref, ids_ref, rank_ref, o_ref):
    ids = ids_ref[...]
    pos = rank_ref[...]
    for ex in range(N_EXPERTS):
        pos = pos + jnp.where(ids == ex, off_ref[ex], 0)
    o_ref[...] = pos


def _positions(offsets, ids, ranks):
    return pl.pallas_call(
        _pos_kernel,
        in_specs=[
            pl.BlockSpec(memory_space=pltpu.SMEM),
            pl.BlockSpec(memory_space=pltpu.VMEM),
            pl.BlockSpec(memory_space=pltpu.VMEM),
        ],
        out_specs=pl.BlockSpec(memory_space=pltpu.VMEM),
        out_shape=jax.ShapeDtypeStruct(ids.shape, I32),
        name="positions",
    )(offsets, ids, ranks)


def _dispatch_kernel(pad_ref, pos_ref, h_ref, xs_ref, zeros, sem, pad_sem, *, tm):
    bt = h_ref.shape[0]

    @pl.when(pl.program_id(0) == 0)
    def _():
        zeros[...] = jnp.zeros_like(zeros)
        group_bits = [1 << b for b in reversed(range((tm // SUBLANES - 1).bit_length()))]

        def pad_copies(ex):
            start, gap = pad_ref[0, ex], pad_ref[1, ex]
            end = start + gap
            groups, singles = gap >> 3, gap & (SUBLANES - 1)
            out = []
            for j in range(SUBLANES - 1):
                out.append((j < singles, pltpu.make_async_copy(
                    zeros.at[pl.ds(0, 1)], xs_ref.at[pl.ds(start + j, 1)], pad_sem)))
            for bit in group_bits:
                first = pl.multiple_of(end - SUBLANES * (groups & (2 * bit - 1)), SUBLANES)
                out.append(((groups & bit) != 0, pltpu.make_async_copy(
                    zeros.at[pl.ds(0, SUBLANES * bit)], xs_ref.at[pl.ds(first, SUBLANES * bit)], pad_sem)))
            return out

        def issue_pad(ex, _):
            for cond, cp in pad_copies(ex):
                @pl.when(cond)
                def _():
                    cp.start()
            return 0

        def drain_pad(ex, _):
            for cond, cp in pad_copies(ex):
                @pl.when(cond)
                def _():
                    cp.wait()
            return 0

        lax.fori_loop(0, N_EXPERTS, issue_pad, 0)
        lax.fori_loop(0, N_EXPERTS, drain_pad, 0)

        zrows = zeros.shape[0]
        first = (pad_ref[0, N_EXPERTS - 1] + pad_ref[1, N_EXPERTS - 1]) // zrows

        def tail_copy(g):
            return pltpu.make_async_copy(zeros, xs_ref.at[pl.ds(pl.multiple_of(g * zrows, zrows), zrows)],
                                         pad_sem)

        def issue_tail(g, _):
            tail_copy(g).start()
            return 0

        def drain_tail(g, _):
            tail_copy(g).wait()
            return 0

        lax.fori_loop(first, xs_ref.shape[0] // zrows, issue_tail, 0)
        lax.fori_loop(first, xs_ref.shape[0] // zrows, drain_tail, 0)

    def issue(r, _):
        for k in range(TOP_K):
            p = pos_ref[0, k, r]
            pltpu.make_async_copy(h_ref.at[pl.ds(r, 1)], xs_ref.at[pl.ds(p, 1)], sem).start(
                priority=k % DMA_QUEUES)
        return 0

    lax.fori_loop(0, bt, issue, 0, unroll=ROW_DMA_UNROLL)
    for _ in range(TOP_K):
        pltpu.make_async_copy(h_ref, xs_ref.at[pl.ds(0, bt)], sem).wait()


def _dispatch(hp, pos3, pads, n_rows, bt, tm):
    n, c = hp.shape
    return pl.pallas_call(
        functools.partial(_dispatch_kernel, tm=tm),
        grid=(n // bt,),
        in_specs=[
            pl.BlockSpec(memory_space=pltpu.SMEM),
            pl.BlockSpec((1, K_PAD, bt), lambda i: (i, 0, 0), memory_space=pltpu.SMEM),
            pl.BlockSpec((bt, c), lambda i: (i, 0)),
        ],
        out_specs=pl.BlockSpec(memory_space=pl.ANY),
        out_shape=jax.ShapeDtypeStruct((n_rows, c), U32),
        scratch_shapes=[pltpu.VMEM((tm // 2, c), U32), pltpu.SemaphoreType.DMA(()),
                        pltpu.SemaphoreType.DMA(())],
        compiler_params=_cparams(("arbitrary",)),
        name="dispatch",
    )(pads, pos3, hp)


def _expert_tiles(ts_ref, src_ref, dst_ref, in_buf, out_buf, zero_buf, in_sem, out_sem, zero_sem, compute):
    e = pl.program_id(0)
    n_exp = pl.num_programs(0)
    tm = in_buf.shape[1]
    n_tiles = dst_ref.shape[0] // tm
    n_used = ts_ref[n_exp]

    def rows(g):
        return pl.ds(pl.multiple_of(g * tm, tm), tm)

    def load(g):
        return pltpu.make_async_copy(src_ref.at[rows(g)], in_buf.at[g % 2], in_sem.at[g % 2])

    def store(g):
        return pltpu.make_async_copy(out_buf.at[g % 2], dst_ref.at[rows(g)], out_sem.at[g % 2])

    @pl.when(e == 0)
    def _():
        load(0).start(priority=TILE_DMA_PRIORITY)

    def tile(g, _):
        @pl.when(g + 1 < n_used)
        def _():
            load(g + 1).start(priority=TILE_DMA_PRIORITY)

        load(g).wait()

        @pl.when(g >= 2)
        def _():
            store(g - 2).wait()

        out_buf[g % 2] = compute(in_buf[g % 2])
        store(g).start(priority=TILE_DMA_PRIORITY)
        return 0

    lax.fori_loop(ts_ref[e], ts_ref[e + 1], tile, 0)

    @pl.when(e == n_exp - 1)
    def _():
        @pl.when(n_used >= 2)
        def _():
            store(n_used - 2).wait()

        store(n_used - 1).wait()

        zero_buf[...] = jnp.zeros_like(zero_buf)

        def fill(g):
            return pltpu.make_async_copy(zero_buf, dst_ref.at[rows(g)], zero_sem)

        def start_fill(g, _):
            fill(g).start()
            return 0

        def wait_fill(g, _):
            fill(g).wait()
            return 0

        lax.fori_loop(n_used, n_tiles, start_fill, 0)
        lax.fori_loop(n_used, n_tiles, wait_fill, 0)


def _expert_up_kernel(ts_ref, wg_ref, wu_ref, xs_ref, act_ref, w_scr, in_buf, out_buf, zero_buf,
                      in_sem, out_sem, zero_sem):
    e = pl.program_id(0)
    f = wg_ref.shape[2]

    @pl.when(ts_ref[e + 1] > ts_ref[e])
    def _():
        w_scr[:, :f] = wg_ref[0].astype(BF16)
        w_scr[:, f:] = wu_ref[0].astype(BF16)

    def compute(xp):
        lo, hi = _unpack_halves(xp)
        x = jnp.concatenate([lo.astype(BF16), hi.astype(BF16)], axis=1)
        gu = _dot(x, w_scr[...])
        g, u = gu[:, :f], gu[:, f:]
        return (g * jax.nn.sigmoid(g) * u).astype(BF16)

    _expert_tiles(ts_ref, xs_ref, act_ref, in_buf, out_buf, zero_buf, in_sem, out_sem, zero_sem, compute)


def _expert_up(tile_start, xs, w_gate_e, w_up_e, tm):
    p, c = xs.shape
    e, d, f = w_gate_e.shape
    grid_spec = pltpu.PrefetchScalarGridSpec(
        num_scalar_prefetch=1,
        grid=(e,),
        in_specs=[
            pl.BlockSpec((1, d, f), lambda i, ts: (i, 0, 0)),
            pl.BlockSpec((1, d, f), lambda i, ts: (i, 0, 0)),
            pl.BlockSpec(memory_space=pl.ANY),
        ],
        out_specs=pl.BlockSpec(memory_space=pl.ANY),
        scratch_shapes=[
            pltpu.VMEM((d, 2 * f), BF16),
            pltpu.VMEM((2, tm, c), U32),
            pltpu.VMEM((2, tm, f), BF16),
            pltpu.VMEM((tm, f), BF16),
            pltpu.SemaphoreType.DMA((2,)),
            pltpu.SemaphoreType.DMA((2,)),
            pltpu.SemaphoreType.DMA(()),
        ],
    )
    return pl.pallas_call(
        _expert_up_kernel,
        grid_spec=grid_spec,
        out_shape=jax.ShapeDtypeStruct((p, f), BF16),
        compiler_params=_cparams(("arbitrary",)),
        name="expert_up",
    )(tile_start, w_gate_e, w_up_e, xs)


def _expert_down_kernel(ts_ref, wd_ref, act_ref, ys_ref, w_scr, in_buf, out_buf, zero_buf,
                        in_sem, out_sem, zero_sem):
    e = pl.program_id(0)

    @pl.when(ts_ref[e + 1] > ts_ref[e])
    def _():
        w_scr[...] = wd_ref[0].astype(BF16)

    def compute(a):
        return _pack_halves(_dot(a, w_scr[...]))

    _expert_tiles(ts_ref, act_ref, ys_ref, in_buf, out_buf, zero_buf, in_sem, out_sem, zero_sem, compute)


def _expert_down(tile_start, act, w_down_e, tm):
    p, f = act.shape
    e, _, d = w_down_e.shape
    grid_spec = pltpu.PrefetchScalarGridSpec(
        num_scalar_prefetch=1,
        grid=(e,),
        in_specs=[
            pl.BlockSpec((1, f, d), lambda i, ts: (i, 0, 0)),
            pl.BlockSpec(memory_space=pl.ANY),
        ],
        out_specs=pl.BlockSpec(memory_space=pl.ANY),
        scratch_shapes=[
            pltpu.VMEM((f, d), BF16),
            pltpu.VMEM((2, tm, f), BF16),
            pltpu.VMEM((2, tm, d // 2), U32),
            pltpu.VMEM((tm, d // 2), U32),
            pltpu.SemaphoreType.DMA((2,)),
            pltpu.SemaphoreType.DMA((2,)),
            pltpu.SemaphoreType.DMA(()),
        ],
    )
    return pl.pallas_call(
        _expert_down_kernel,
        grid_spec=grid_spec,
        out_shape=jax.ShapeDtypeStruct((p, d // 2), U32),
        compiler_params=_cparams(("arbitrary",)),
        name="expert_down",
    )(tile_start, w_down_e, act)


def _shared_up_kernel(h_ref, wg_ref, wu_ref, o_ref):
    h = h_ref[...]
    g = _dot(h, wg_ref[...].astype(BF16))
    u = _dot(h, wu_ref[...].astype(BF16))
    o_ref[...] = (g * jax.nn.sigmoid(g) * u).astype(o_ref.dtype)


def _shared_up(h, w_gate_s, w_up_s, bm=1024, bn=256):
    n, d = h.shape
    f = w_gate_s.shape[1]
    return pl.pallas_call(
        _shared_up_kernel,
        grid=(n // bm, f // bn),
        in_specs=[
            pl.BlockSpec((bm, d), lambda i, j: (i, 0)),
            pl.BlockSpec((d, bn), lambda i, j: (0, j)),
            pl.BlockSpec((d, bn), lambda i, j: (0, j)),
        ],
        out_specs=pl.BlockSpec((bm, bn), lambda i, j: (i, j)),
        out_shape=jax.ShapeDtypeStruct((n, f), BF16),
        compiler_params=_cparams(("arbitrary", "arbitrary")),
        name="shared_up",
    )(h, w_gate_s, w_up_s)


def _combine_kernel(pos_ref, pos_next_ref, x_ref, s_ref, wt_ref, mod_ref, g_ref, ys_ref, wd_ref, o_ref,
                    ybuf, wd_scr, stage, sem, wd_sem):
    i = pl.program_id(0)
    bt = x_ref.shape[0]
    slot = i % 2

    @pl.when(i == 0)
    def _():
        rows = stage.shape[0]
        for part in range(wd_scr.shape[0] // rows):
            cp = pltpu.make_async_copy(wd_ref.at[pl.ds(part * rows, rows)], stage, wd_sem)
            cp.start()
            cp.wait()
            wd_scr[part * rows:(part + 1) * rows, :] = stage[...].astype(BF16)

    def gather(p_ref, dst_slot):
        def issue(r, _):
            for k in range(TOP_K):
                p = p_ref[0, k, r]
                pltpu.make_async_copy(ys_ref.at[pl.ds(p, 1)], ybuf.at[dst_slot, k, pl.ds(r, 1)],
                                      sem.at[dst_slot]).start(priority=k % DMA_QUEUES)
            return 0

        lax.fori_loop(0, bt, issue, 0, unroll=ROW_DMA_UNROLL)

    @pl.when(i == 0)
    def _():
        gather(pos_ref, 0)

    @pl.when(i + 1 < pl.num_programs(0))
    def _():
        gather(pos_next_ref, 1 - slot)

    for k in range(TOP_K):
        pltpu.make_async_copy(ys_ref.at[pl.ds(0, bt)], ybuf.at[slot, k], sem.at[slot]).wait()

    c = ybuf.shape[3]
    acc_lo = jnp.zeros((bt, c), F32)
    acc_hi = jnp.zeros((bt, c), F32)
    for k in range(TOP_K):
        lo, hi = _unpack_halves(ybuf[slot, k])
        wk = wt_ref[:, k:k + 1]
        acc_lo = acc_lo + wk * lo
        acc_hi = acc_hi + wk * hi
    moe = jnp.concatenate([acc_lo, acc_hi], axis=1) + _dot(s_ref[...], wd_scr[...])
    gate = mod_ref[0][5:6]
    x = x_ref[...] + gate * moe
    y = x * lax.rsqrt(jnp.mean(x * x, axis=-1, keepdims=True) + EPS) * g_ref[...]
    o_ref[...] = y


def _combine(pos3, x1, sact, wts_t, mod, final_g, ys, w_down_s, seq, bt, stage_rows=128):
    n, d = x1.shape
    f = sact.shape[1]
    nb = n // bt
    pos_spec = lambda imap: pl.BlockSpec((1, K_PAD, bt), imap, memory_space=pltpu.SMEM)
    return pl.pallas_call(
        _combine_kernel,
        grid=(nb,),
        in_specs=[
            pos_spec(lambda i: (i, 0, 0)),
            pos_spec(lambda i: (jnp.minimum(i + 1, nb - 1), 0, 0)),
            pl.BlockSpec((bt, d), lambda i: (i, 0)),
            pl.BlockSpec((bt, f), lambda i: (i, 0)),
            pl.BlockSpec((bt, K_PAD), lambda i: (i, 0)),
            pl.BlockSpec((1, N_MOD, d), lambda i: (i * bt // seq, 0, 0)),
            pl.BlockSpec((1, d), lambda i: (0, 0)),
            pl.BlockSpec(memory_space=pl.ANY),
            pl.BlockSpec(memory_space=pl.ANY),
        ],
        out_specs=pl.BlockSpec((bt, d), lambda i: (i, 0)),
        out_shape=jax.ShapeDtypeStruct((n, d), F32),
        scratch_shapes=[
            pltpu.VMEM((2, TOP_K, bt, d // 2), U32),
            pltpu.VMEM((f, d), BF16),
            pltpu.VMEM((stage_rows, d), F32),
            pltpu.SemaphoreType.DMA((2,)),
            pltpu.SemaphoreType.DMA(()),
        ],
        compiler_params=_cparams(("arbitrary",)),
        name="combine",
    )(pos3, pos3, x1, sact, wts_t, mod, final_g.reshape(1, d), ys, w_down_s)


def _moe_tiles(counts, n_tokens, tm):
    e = counts.shape[0]
    n_tiles = n_tokens * TOP_K // tm + e
    tiles_per = (counts + tm - 1) // tm
    tile_end = jnp.cumsum(tiles_per)
    tile_start = jnp.concatenate([jnp.zeros((1,), I32), tile_end.astype(I32)])
    offsets = tile_start[:-1] * tm
    pads = jnp.stack([offsets + counts, tiles_per * tm - counts]).astype(I32)
    return offsets, tile_start, pads, n_tiles


def kernel(x, c, w_ada, b_ada, norm1_g, w_in, w_s, b_s, w_up_a, w_up_b, w_o, norm2_g, w_router,
           router_bias, w_gate_e, w_up_e, w_down_e, w_gate_s, w_up_s, w_down_s, final_g):
    batch, seq, d = x.shape
    n = batch * seq
    depth = w_ada.shape[0]
    a_width = w_up_a.shape[1]
    b_width = w_up_b.shape[1]
    n_heads = b_width // HEAD_DIM
    tm = 256
    bt_dispatch = 256
    bt_combine = 128

    assert depth == 1, "the combine kernel applies the final norm, so only one layer is supported"
    l = 0
    x2 = x.reshape(n, d)
    mod = _ada(c, w_ada[l], b_ada[l])
    h1 = _norm1(x2, norm1_g[l], mod, seq)
    proj = _inproj(h1, w_in[l], a_width, b_width)
    sgu = _sgu(proj, w_s[l], b_s[l], a_width)
    att = _attention(proj, batch, seq, 2 * a_width, 2 * a_width + b_width,
                     2 * a_width + 2 * b_width, n_heads)
    merged = _merge(sgu, att, w_up_a[l], w_up_b[l], proj,
                    2 * a_width + 3 * b_width, 2 * a_width + 3 * b_width + d)
    x1 = _resid_matmul(merged, w_o[l], x2, mod, 2, seq, "out_proj")

    h2, h2p, ids, wts, ranks, counts = _router(x1, norm2_g[l], mod, w_router[l], router_bias[l], seq)
    offsets, tile_start, pads, n_tiles = _moe_tiles(counts[:, 0], n, tm)
    pos = _positions(offsets, ids, ranks)
    blocked = lambda b: pos.reshape(K_PAD, n // b, b).transpose(1, 0, 2)
    xs = _dispatch(h2p, blocked(bt_dispatch), pads, n_tiles * tm, bt_dispatch, tm)
    act = _expert_up(tile_start, xs, w_gate_e[l], w_up_e[l], tm)
    ys = _expert_down(tile_start, act, w_down_e[l], tm)
    sact = _shared_up(h2, w_gate_s[l], w_up_s[l])
    out = _combine(blocked(bt_combine), x1, sact, wts.T, mod, final_g, ys, w_down_s[l], seq, bt_combine)
    return out.reshape(batch, seq, d)
```

```python
import functools

import jax
import jax.numpy as jnp
from jax import lax
from jax.experimental import pallas as pl
from jax.experimental.pallas import tpu as pltpu

F32 = jnp.float32
BF16 = jnp.bfloat16
I32 = jnp.int32
U32 = jnp.uint32

EPS = 1e-6
A_GROUPS = 8
A_CHUNK = 128
HEAD_DIM = 128
N_EXPERTS = 64
TOP_K = 6
N_GROUPS = 8
TOPK_GROUPS = 4
ROUTED_SCALE = 2.5
N_MOD = 6
SUBLANES = 8
K_PAD = SUBLANES
ROW_DMA_UNROLL = 8
DMA_QUEUES = 2
TILE_DMA_PRIORITY = 1
DEAD_LOG = -105.0

VMEM_LIMIT = 56 * 1024 * 1024


def _cparams(sem):
    return pltpu.CompilerParams(dimension_semantics=sem, vmem_limit_bytes=VMEM_LIMIT)


def _dot(a, b):
    return jnp.dot(a, b, preferred_element_type=F32)


def _dot_nt(a, b):
    return lax.dot_general(a, b, (((1,), (1,)), ((), ())), preferred_element_type=F32)


def _pack_halves(h):
    c = h.shape[1] // 2
    lo = pltpu.bitcast(h[:, :c].astype(BF16).astype(F32), U32)
    hi = pltpu.bitcast(h[:, c:].astype(BF16).astype(F32), U32)
    return (hi & jnp.uint32(0xFFFF0000)) | (lo >> 16)


def _unpack_halves(p):
    lo = pltpu.bitcast(p << 16, F32)
    hi = pltpu.bitcast(p & jnp.uint32(0xFFFF0000), F32)
    return lo, hi


def _ada_kernel(c_ref, w_ref, b_ref, o_ref):
    c = c_ref[...]
    cact = (c * jax.nn.sigmoid(c)).astype(BF16)
    o_ref[...] = _dot(cact, w_ref[...].astype(BF16)) + b_ref[...]


def _ada(c, w_ada, b_ada, bn=512):
    b, d = c.shape
    n = w_ada.shape[1]
    cp = jnp.zeros((8, d), F32).at[:b].set(c)
    out = pl.pallas_call(
        _ada_kernel,
        grid=(n // bn,),
        in_specs=[
            pl.BlockSpec((8, d), lambda j: (0, 0)),
            pl.BlockSpec((d, bn), lambda j: (0, j)),
            pl.BlockSpec((1, bn), lambda j: (0, j)),
        ],
        out_specs=pl.BlockSpec((8, bn), lambda j: (0, j)),
        out_shape=jax.ShapeDtypeStruct((8, n), F32),
        compiler_params=_cparams(("arbitrary",)),
        name="ada",
    )(cp, w_ada, b_ada.reshape(1, n))
    return out[:b].reshape(b, N_MOD, d)


def _norm_mod(x, g_ref, mod_ref, shift_idx, scale_idx):
    y = x * lax.rsqrt(jnp.mean(x * x, axis=-1, keepdims=True) + EPS) * g_ref[...]
    mod = mod_ref[0]
    return y * (1.0 + mod[scale_idx:scale_idx + 1]) + mod[shift_idx:shift_idx + 1]


def _norm1_kernel(x_ref, g_ref, mod_ref, o_ref):
    o_ref[...] = _norm_mod(x_ref[...], g_ref, mod_ref, 0, 1).astype(BF16)


def _norm1(x2, g, mod, seq, bm=512):
    n, d = x2.shape
    return pl.pallas_call(
        _norm1_kernel,
        grid=(n // bm,),
        in_specs=[
            pl.BlockSpec((bm, d), lambda i: (i, 0)),
            pl.BlockSpec((1, d), lambda i: (0, 0)),
            pl.BlockSpec((1, N_MOD, d), lambda i: (i * bm // seq, 0, 0)),
        ],
        out_specs=pl.BlockSpec((bm, d), lambda i: (i, 0)),
        out_shape=jax.ShapeDtypeStruct((n, d), BF16),
        compiler_params=_cparams(("arbitrary",)),
        name="norm1",
    )(x2, g.reshape(1, d), mod)


def _inproj_kernel(h_ref, w_ref, o_ref, *, n_gelu, n_ln, n_plain, group):
    j = pl.program_id(1)

    def run(epilogue):
        for s in range(o_ref.shape[1] // group):
            cs = slice(s * group, (s + 1) * group)
            acc = _dot(h_ref[...], w_ref[:, cs].astype(BF16))
            o_ref[:, cs] = epilogue(acc).astype(o_ref.dtype)

    def group_norm(acc):
        a = jax.nn.gelu(acc)
        cen = a - jnp.mean(a, axis=-1, keepdims=True)
        var = jnp.mean(cen * cen, axis=-1, keepdims=True)
        return cen * lax.rsqrt(var + EPS)

    @pl.when(j < n_gelu)
    def _():
        run(jax.nn.gelu)

    @pl.when((j >= n_gelu) & (j < n_ln))
    def _():
        run(group_norm)

    @pl.when((j >= n_ln) & (j < n_plain))
    def _():
        run(lambda acc: acc)

    @pl.when(j >= n_plain)
    def _():
        run(jax.nn.sigmoid)


def _inproj(h, w_in, a_width, b_width, bm=1024, bn=512):
    n, d = h.shape
    cols = w_in.shape[1]
    group = a_width // A_GROUPS
    kern = functools.partial(
        _inproj_kernel,
        n_gelu=a_width // bn,
        n_ln=2 * a_width // bn,
        n_plain=(2 * a_width + 3 * b_width) // bn,
        group=group,
    )
    return pl.pallas_call(
        kern,
        grid=(n // bm, cols // bn),
        in_specs=[
            pl.BlockSpec((bm, d), lambda i, j: (i, 0)),
            pl.BlockSpec((d, bn), lambda i, j: (0, j)),
        ],
        out_specs=pl.BlockSpec((bm, bn), lambda i, j: (i, j)),
        out_shape=jax.ShapeDtypeStruct((n, cols), BF16),
        compiler_params=_cparams(("arbitrary", "arbitrary")),
        name="inproj",
    )(h, w_in)


def _sgu_kernel(u_ref, v_ref, ws_ref, bs_ref, o_ref, *, group):
    t = A_CHUNK
    row = lax.broadcasted_iota(I32, (t, t), 0)
    col = lax.broadcasted_iota(I32, (t, t), 1)
    for g in range(A_GROUPS):
        ws = jnp.where(row >= col, ws_ref[g], 0.0).astype(BF16)
        bias = bs_ref[:, g:g + 1]
        for c in range(u_ref.shape[0] // t):
            rs = slice(c * t, (c + 1) * t)
            cs = slice(g * group, (g + 1) * group)
            mixed = _dot(ws, v_ref[rs, cs]) + bias
            o_ref[rs, cs] = (u_ref[rs, cs].astype(F32) * mixed).astype(o_ref.dtype)


def _sgu(proj, w_s, b_s, a_width, bm=512):
    n = proj.shape[0]
    group = a_width // A_GROUPS
    return pl.pallas_call(
        functools.partial(_sgu_kernel, group=group),
        grid=(n // bm,),
        in_specs=[
            pl.BlockSpec((bm, a_width), lambda i: (i, 0)),
            pl.BlockSpec((bm, a_width), lambda i: (i, 1)),
            pl.BlockSpec((A_GROUPS, A_CHUNK, A_CHUNK), lambda i: (0, 0, 0)),
            pl.BlockSpec((A_CHUNK, A_GROUPS), lambda i: (0, 0)),
        ],
        out_specs=pl.BlockSpec((bm, a_width), lambda i: (i, 0)),
        out_shape=jax.ShapeDtypeStruct((n, a_width), BF16),
        compiler_params=_cparams(("arbitrary",)),
        name="sgu",
    )(proj, proj, w_s, b_s.T)


def _attn_kernel(q_ref, k_ref, v_ref, o_ref, *, blk, heads, scale):
    seq = q_ref.shape[0]
    row = lax.broadcasted_iota(I32, (blk, blk), 0)
    col = lax.broadcasted_iota(I32, (blk, blk), 1)
    upper = jnp.where(row > col, 1.0, 0.0).astype(BF16)
    strict = col < row

    hslices = [slice(h * HEAD_DIM, (h + 1) * HEAD_DIM) for h in range(heads)]

    def update(qv, ks, cs, accs, diagonal):
        zs = [_dot_nt(qv[h], k_ref[pl.ds(ks, blk), hslices[h]]) * scale for h in range(heads)]
        sps = [jnp.maximum(z, 0.0) + jnp.log(1.0 + jnp.exp(-jnp.abs(z))) for z in zs]
        stays = [jnp.where(strict, -sp, 0.0) if diagonal else -sp for sp in sps]
        his = [s.astype(BF16) for s in stays]
        los = [(s - hi.astype(F32)).astype(BF16) for s, hi in zip(stays, his)]
        laters = [_dot(hi, upper) + _dot(lo, upper) for hi, lo in zip(his, los)]
        ws = [jnp.exp((zs[h] - sps[h]) + laters[h] + cs[h]) for h in range(heads)]
        if diagonal:
            ws = [jnp.where(strict, w, 0.0) for w in ws]
        accs = tuple(accs[h] + _dot(ws[h].astype(BF16), v_ref[pl.ds(ks, blk), hslices[h]])
                     for h in range(heads))
        cs = tuple(cs[h] + laters[h][:, :1] + stays[h][:, :1] for h in range(heads))
        return cs, accs

    def q_block(qi, _):
        qs = pl.multiple_of(qi * blk, blk)
        qv = [q_ref[pl.ds(qs, blk), hs] for hs in hslices]
        cs, accs = update(qv, qs, (jnp.zeros((blk, 1), F32),) * heads,
                          (jnp.zeros((blk, HEAD_DIM), F32),) * heads, True)

        def live(carry):
            n, cs, _ = carry
            cmax = jnp.max(cs[0])
            for c in cs[1:]:
                cmax = jnp.maximum(cmax, jnp.max(c))
            return (n <= qi) & (cmax > DEAD_LOG)

        def older_block(carry):
            n, cs, accs = carry
            ks = pl.multiple_of((qi - n) * blk, blk)
            cs, accs = update(qv, ks, cs, accs, False)
            return n + 1, cs, accs

        _, _, accs = lax.while_loop(live, older_block, (jnp.int32(1), cs, accs))
        for h in range(heads):
            o_ref[pl.ds(qs, blk), hslices[h]] = accs[h].astype(o_ref.dtype)
        return 0

    lax.fori_loop(0, seq // blk, q_block, 0)


def _attention(proj, batch, seq, q_col, k_col, v_col, n_heads, blk=256, heads=4):
    width = heads * HEAD_DIM
    qc, kc, vc = q_col // width, k_col // width, v_col // width
    return pl.pallas_call(
        functools.partial(_attn_kernel, blk=blk, heads=heads, scale=HEAD_DIM ** -0.5),
        grid=(batch, n_heads // heads),
        in_specs=[
            pl.BlockSpec((seq, width), lambda b, h: (b, qc + h)),
            pl.BlockSpec((seq, width), lambda b, h: (b, kc + h)),
            pl.BlockSpec((seq, width), lambda b, h: (b, vc + h)),
        ],
        out_specs=pl.BlockSpec((seq, width), lambda b, h: (b, h)),
        out_shape=jax.ShapeDtypeStruct((batch * seq, n_heads * HEAD_DIM), BF16),
        compiler_params=_cparams(("arbitrary", "arbitrary")),
        name="attn",
    )(proj, proj, proj)


def _merge_kernel(a_ref, b_ref, wa_ref, wb_ref, ga_ref, gb_ref, o_ref):
    ya = _dot(a_ref[...], wa_ref[...].astype(BF16))
    yb = _dot(b_ref[...], wb_ref[...].astype(BF16))
    o_ref[...] = (ga_ref[...].astype(F32) * ya + gb_ref[...].astype(F32) * yb).astype(o_ref.dtype)


def _merge(sgu, att, w_up_a, w_up_b, proj, ga_col, gb_col, bm=1024, bn=512):
    n, ka = sgu.shape
    kb = att.shape[1]
    d = w_up_a.shape[1]
    gac, gbc = ga_col // bn, gb_col // bn
    return pl.pallas_call(
        _merge_kernel,
        grid=(n // bm, d // bn),
        in_specs=[
            pl.BlockSpec((bm, ka), lambda i, j: (i, 0)),
            pl.BlockSpec((bm, kb), lambda i, j: (i, 0)),
            pl.BlockSpec((ka, bn), lambda i, j: (0, j)),
            pl.BlockSpec((kb, bn), lambda i, j: (0, j)),
            pl.BlockSpec((bm, bn), lambda i, j: (i, gac + j)),
            pl.BlockSpec((bm, bn), lambda i, j: (i, gbc + j)),
        ],
        out_specs=pl.BlockSpec((bm, bn), lambda i, j: (i, j)),
        out_shape=jax.ShapeDtypeStruct((n, d), BF16),
        compiler_params=_cparams(("arbitrary", "arbitrary")),
        name="merge",
    )(sgu, att, w_up_a, w_up_b, proj, proj)


def _resid_kernel(a_ref, w_ref, x_ref, mod_ref, o_ref, *, gate_idx):
    y = _dot(a_ref[...], w_ref[...].astype(BF16))
    gate = mod_ref[0][gate_idx:gate_idx + 1]
    o_ref[...] = x_ref[...] + gate * y


def _resid_matmul(a, w, x2, mod, gate_idx, seq, name, bm=1024, bn=512):
    n, k = a.shape
    d = w.shape[1]
    return pl.pallas_call(
        functools.partial(_resid_kernel, gate_idx=gate_idx),
        grid=(n // bm, d // bn),
        in_specs=[
            pl.BlockSpec((bm, k), lambda i, j: (i, 0)),
            pl.BlockSpec((k, bn), lambda i, j: (0, j)),
            pl.BlockSpec((bm, bn), lambda i, j: (i, j)),
            pl.BlockSpec((1, N_MOD, bn), lambda i, j: (i * bm // seq, 0, j)),
        ],
        out_specs=pl.BlockSpec((bm, bn), lambda i, j: (i, j)),
        out_shape=jax.ShapeDtypeStruct((n, d), F32),
        compiler_params=_cparams(("arbitrary", "arbitrary")),
        name=name,
    )(a, w, x2, mod)


def _first_argmax(vals, iota, size):
    m = jnp.max(vals, axis=0, keepdims=True)
    idx = jnp.min(jnp.where(vals == m, iota, size), axis=0, keepdims=True)
    return m, idx


def _router_kernel(x_ref, g_ref, mod_ref, wr_ref, rb_ref, h_ref, hp_ref, ids_ref, wts_ref, rank_ref,
                   cnt_ref, carry_ref):
    i = pl.program_id(0)
    bm = x_ref.shape[0]
    e = N_EXPERTS
    per = e // N_GROUPS

    @pl.when(i == 0)
    def _():
        carry_ref[...] = jnp.zeros_like(carry_ref)

    h = _norm_mod(x_ref[...], g_ref, mod_ref, 3, 4)
    h_ref[...] = h.astype(BF16)
    hp_ref[...] = _pack_halves(h)

    h_hi = h.astype(BF16)
    h_lo = (h - h_hi.astype(F32)).astype(BF16)
    wr = wr_ref[...]
    w_hi = wr.astype(BF16)
    w_lo = (wr - w_hi.astype(F32)).astype(BF16)
    logits = _dot_nt(w_hi, h_hi) + (_dot_nt(w_hi, h_lo) + _dot_nt(w_lo, h_hi))
    scores = jax.nn.sigmoid(logits)
    choice = scores + rb_ref[...]

    iota_p = lax.broadcasted_iota(I32, (per, bm), 0)
    gs_rows = []
    for g in range(N_GROUPS):
        cg = choice[g * per:(g + 1) * per, :]
        m1, i1 = _first_argmax(cg, iota_p, per)
        m2 = jnp.max(jnp.where(iota_p == i1, -jnp.inf, cg), axis=0, keepdims=True)
        gs_rows.append(m1 + m2)
    gs = jnp.concatenate(gs_rows, axis=0)

    iota_g = lax.broadcasted_iota(I32, (N_GROUPS, bm), 0)
    gsel = jnp.zeros((N_GROUPS, bm), F32)
    for _ in range(TOPK_GROUPS):
        _, ig = _first_argmax(gs, iota_g, N_GROUPS)
        hit = iota_g == ig
        gsel = jnp.where(hit, 1.0, gsel)
        gs = jnp.where(hit, -jnp.inf, gs)
    emask = jnp.concatenate(
        [jnp.broadcast_to(gsel[g:g + 1, :], (per, bm)) for g in range(N_GROUPS)], axis=0)

    iota_e = lax.broadcasted_iota(I32, (e, bm), 0)
    masked = jnp.where(emask > 0.5, choice, -jnp.inf)
    self_f = jnp.zeros((e, bm), F32)
    idx_rows, w_rows = [], []
    for _ in range(TOP_K):
        _, ie = _first_argmax(masked, iota_e, e)
        hit = iota_e == ie
        self_f = jnp.where(hit, 1.0, self_f)
        idx_rows.append(ie)
        w_rows.append(jnp.sum(jnp.where(hit, scores, 0.0), axis=0, keepdims=True))
        masked = jnp.where(hit, -jnp.inf, masked)
    w = jnp.concatenate(w_rows, axis=0)
    w = w / jnp.sum(w, axis=0, keepdims=True) * ROUTED_SCALE

    t_row = lax.broadcasted_iota(I32, (bm, bm), 0)
    t_col = lax.broadcasted_iota(I32, (bm, bm), 1)
    before = jnp.where(t_row < t_col, 1.0, 0.0).astype(BF16)
    rank_mat = _dot(self_f.astype(BF16), before) + carry_ref[:, :1]
    rank_rows = [jnp.sum(jnp.where(iota_e == ie, rank_mat, 0.0), axis=0, keepdims=True)
                 for ie in idx_rows]
    carry_ref[...] = carry_ref[...] + jnp.sum(self_f, axis=1, keepdims=True)

    pad_i = jnp.zeros((K_PAD - TOP_K, bm), I32)
    ids_ref[...] = jnp.concatenate(idx_rows + [pad_i], axis=0)
    wts_ref[...] = jnp.concatenate([w, jnp.zeros((K_PAD - TOP_K, bm), F32)], axis=0)
    rank_ref[...] = jnp.concatenate([r.astype(I32) for r in rank_rows] + [pad_i], axis=0)
    cnt_ref[...] = carry_ref[...].astype(I32)


def _router(x1, g, mod, w_router, router_bias, seq, bm=512):
    n, d = x1.shape
    e = N_EXPERTS
    tok = lambda i: (0, i)
    return pl.pallas_call(
        _router_kernel,
        grid=(n // bm,),
        in_specs=[
            pl.BlockSpec((bm, d), lambda i: (i, 0)),
            pl.BlockSpec((1, d), lambda i: (0, 0)),
            pl.BlockSpec((1, N_MOD, d), lambda i: (i * bm // seq, 0, 0)),
            pl.BlockSpec((e, d), lambda i: (0, 0)),
            pl.BlockSpec((e, 1), lambda i: (0, 0)),
        ],
        out_specs=[
            pl.BlockSpec((bm, d), lambda i: (i, 0)),
            pl.BlockSpec((bm, d // 2), lambda i: (i, 0)),
            pl.BlockSpec((K_PAD, bm), tok),
            pl.BlockSpec((K_PAD, bm), tok),
            pl.BlockSpec((K_PAD, bm), tok),
            pl.BlockSpec((e, 128), lambda i: (0, 0)),
        ],
        out_shape=[
            jax.ShapeDtypeStruct((n, d), BF16),
            jax.ShapeDtypeStruct((n, d // 2), U32),
            jax.ShapeDtypeStruct((K_PAD, n), I32),
            jax.ShapeDtypeStruct((K_PAD, n), F32),
            jax.ShapeDtypeStruct((K_PAD, n), I32),
            jax.ShapeDtypeStruct((e, 128), I32),
        ],
        scratch_shapes=[pltpu.VMEM((e, 128), F32)],
        compiler_params=_cparams(("arbitrary",)),
        name="router",
    )(x1, g.reshape(1, d), mod, w_router.T, router_bias.reshape(e, 1))


def _pos_kernel(off_ref, ids_ref, rank_ref, o_ref):
    ids = ids_ref[...]
    pos = rank_ref[...]
    for ex in range(N_EXPERTS):
        pos = pos + jnp.where(ids == ex, off_ref[ex], 0)
    o_ref[...] = pos


def _positions(offsets, ids, ranks):
    return pl.pallas_call(
        _pos_kernel,
        in_specs=[
            pl.BlockSpec(memory_space=pltpu.SMEM),
            pl.BlockSpec(memory_space=pltpu.VMEM),
            pl.BlockSpec(memory_space=pltpu.VMEM),
        ],
        out_specs=pl.BlockSpec(memory_space=pltpu.VMEM),
        out_shape=jax.ShapeDtypeStruct(ids.shape, I32),
        name="positions",
    )(offsets, ids, ranks)


def _dispatch_kernel(pad_ref, pos_ref, h_ref, xs_ref, zeros, sem, pad_sem, *, tm):
    bt = h_ref.shape[0]

    @pl.when(pl.program_id(0) == 0)
    def _():
        zeros[...] = jnp.zeros_like(zeros)
        group_bits = [1 << b for b in reversed(range((tm // SUBLANES - 1).bit_length()))]

        def pad_copies(ex):
            start, gap = pad_ref[0, ex], pad_ref[1, ex]
            end = start + gap
            groups, singles = gap >> 3, gap & (SUBLANES - 1)
            out = []
            for j in range(SUBLANES - 1):
                out.append((j < singles, pltpu.make_async_copy(
                    zeros.at[pl.ds(0, 1)], xs_ref.at[pl.ds(start + j, 1)], pad_sem)))
            for bit in group_bits:
                first = pl.multiple_of(end - SUBLANES * (groups & (2 * bit - 1)), SUBLANES)
                out.append(((groups & bit) != 0, pltpu.make_async_copy(
                    zeros.at[pl.ds(0, SUBLANES * bit)], xs_ref.at[pl.ds(first, SUBLANES * bit)], pad_sem)))
            return out

        def issue_pad(ex, _):
            for cond, cp in pad_copies(ex):
                @pl.when(cond)
                def _():
                    cp.start()
            return 0

        def drain_pad(ex, _):
            for cond, cp in pad_copies(ex):
                @pl.when(cond)
                def _():
                    cp.wait()
            return 0

        lax.fori_loop(0, N_EXPERTS, issue_pad, 0)
        lax.fori_loop(0, N_EXPERTS, drain_pad, 0)

        zrows = zeros.shape[0]
        first = (pad_ref[0, N_EXPERTS - 1] + pad_ref[1, N_EXPERTS - 1]) // zrows

        def tail_copy(g):
            return pltpu.make_async_copy(zeros, xs_ref.at[pl.ds(pl.multiple_of(g * zrows, zrows), zrows)],
                                         pad_sem)

        def issue_tail(g, _):
            tail_copy(g).start()
            return 0

        def drain_tail(g, _):
            tail_copy(g).wait()
            return 0

        lax.fori_loop(first, xs_ref.shape[0] // zrows, issue_tail, 0)
        lax.fori_loop(first, xs_ref.shape[0] // zrows, drain_tail, 0)

    def issue(r, _):
        for k in range(TOP_K):
            p = pos_ref[0, k, r]
            pltpu.make_async_copy(h_ref.at[pl.ds(r, 1)], xs_ref.at[pl.ds(p, 1)], sem).start(
                priority=k % DMA_QUEUES)
        return 0

    lax.fori_loop(0, bt, issue, 0, unroll=ROW_DMA_UNROLL)
    for _ in range(TOP_K):
        pltpu.make_async_copy(h_ref, xs_ref.at[pl.ds(0, bt)], sem).wait()


def _dispatch(hp, pos3, pads, n_rows, bt, tm):
    n, c = hp.shape
    return pl.pallas_call(
        functools.partial(_dispatch_kernel, tm=tm),
        grid=(n // bt,),
        in_specs=[
            pl.BlockSpec(memory_space=pltpu.SMEM),
            pl.BlockSpec((1, K_PAD, bt), lambda i: (i, 0, 0), memory_space=pltpu.SMEM),
            pl.BlockSpec((bt, c), lambda i: (i, 0)),
        ],
        out_specs=pl.BlockSpec(memory_space=pl.ANY),
        out_shape=jax.ShapeDtypeStruct((n_rows, c), U32),
        scratch_shapes=[pltpu.VMEM((tm // 2, c), U32), pltpu.SemaphoreType.DMA(()),
                        pltpu.SemaphoreType.DMA(())],
        compiler_params=_cparams(("arbitrary",)),
        name="dispatch",
    )(pads, pos3, hp)


def _expert_tiles(ts_ref, src_ref, dst_ref, in_buf, out_buf, zero_buf, in_sem, out_sem, zero_sem, compute):
    e = pl.program_id(0)
    n_exp = pl.num_programs(0)
    tm = in_buf.shape[1]
    n_tiles = dst_ref.shape[0] // tm
    n_used = ts_ref[n_exp]

    def rows(g):
        return pl.ds(pl.multiple_of(g * tm, tm), tm)

    def load(g):
        return pltpu.make_async_copy(src_ref.at[rows(g)], in_buf.at[g % 2], in_sem.at[g % 2])

    def store(g):
        return pltpu.make_async_copy(out_buf.at[g % 2], dst_ref.at[rows(g)], out_sem.at[g % 2])

    @pl.when(e == 0)
    def _():
        load(0).start(priority=TILE_DMA_PRIORITY)

    def tile(g, _):
        @pl.when(g + 1 < n_used)
        def _():
            load(g + 1).start(priority=TILE_DMA_PRIORITY)

        load(g).wait()

        @pl.when(g >= 2)
        def _():
            store(g - 2).wait()

        out_buf[g % 2] = compute(in_buf[g % 2])
        store(g).start(priority=TILE_DMA_PRIORITY)
        return 0

    lax.fori_loop(ts_ref[e], ts_ref[e + 1], tile, 0)

    @pl.when(e == n_exp - 1)
    def _():
        @pl.when(n_used >= 2)
        def _():
            store(n_used - 2).wait()

        store(n_used - 1).wait()

        zero_buf[...] = jnp.zeros_like(zero_buf)

        def fill(g):
            return pltpu.make_async_copy(zero_buf, dst_ref.at[rows(g)], zero_sem)

        def start_fill(g, _):
            fill(g).start()
            return 0

        def wait_fill(g, _):
            fill(g).wait()
            return 0

        lax.fori_loop(n_used, n_tiles, start_fill, 0)
        lax.fori_loop(n_used, n_tiles, wait_fill, 0)


def _expert_up_kernel(ts_ref, wg_ref, wu_ref, xs_ref, act_ref, w_scr, in_buf, out_buf, zero_buf,
                      in_sem, out_sem, zero_sem):
    e = pl.program_id(0)
    f = wg_ref.shape[2]

    @pl.when(ts_ref[e + 1] > ts_ref[e])
    def _():
        w_scr[:, :f] = wg_ref[0].astype(BF16)
        w_scr[:, f:] = wu_ref[0].astype(BF16)

    def compute(xp):
        lo, hi = _unpack_halves(xp)
        x = jnp.concatenate([lo.astype(BF16), hi.astype(BF16)], axis=1)
        gu = _dot(x, w_scr[...])
        g, u = gu[:, :f], gu[:, f:]
        return (g * jax.nn.sigmoid(g) * u).astype(BF16)

    _expert_tiles(ts_ref, xs_ref, act_ref, in_buf, out_buf, zero_buf, in_sem, out_sem, zero_sem, compute)


def _expert_up(tile_start, xs, w_gate_e, w_up_e, tm):
    p, c = xs.shape
    e, d, f = w_gate_e.shape
    grid_spec = pltpu.PrefetchScalarGridSpec(
        num_scalar_prefetch=1,
        grid=(e,),
        in_specs=[
            pl.BlockSpec((1, d, f), lambda i, ts: (i, 0, 0)),
            pl.BlockSpec((1, d, f), lambda i, ts: (i, 0, 0)),
            pl.BlockSpec(memory_space=pl.ANY),
        ],
        out_specs=pl.BlockSpec(memory_space=pl.ANY),
        scratch_shapes=[
            pltpu.VMEM((d, 2 * f), BF16),
            pltpu.VMEM((2, tm, c), U32),
            pltpu.VMEM((2, tm, f), BF16),
            pltpu.VMEM((tm, f), BF16),
            pltpu.SemaphoreType.DMA((2,)),
            pltpu.SemaphoreType.DMA((2,)),
            pltpu.SemaphoreType.DMA(()),
        ],
    )
    return pl.pallas_call(
        _expert_up_kernel,
        grid_spec=grid_spec,
        out_shape=jax.ShapeDtypeStruct((p, f), BF16),
        compiler_params=_cparams(("arbitrary",)),
        name="expert_up",
    )(tile_start, w_gate_e, w_up_e, xs)


def _expert_down_kernel(ts_ref, wd_ref, act_ref, ys_ref, w_scr, in_buf, out_buf, zero_buf,
                        in_sem, out_sem, zero_sem):
    e = pl.program_id(0)

    @pl.when(ts_ref[e + 1] > ts_ref[e])
    def _():
        w_scr[...] = wd_ref[0].astype(BF16)

    def compute(a):
        return _pack_halves(_dot(a, w_scr[...]))

    _expert_tiles(ts_ref, act_ref, ys_ref, in_buf, out_buf, zero_buf, in_sem, out_sem, zero_sem, compute)


def _expert_down(tile_start, act, w_down_e, tm):
    p, f = act.shape
    e, _, d = w_down_e.shape
    grid_spec = pltpu.PrefetchScalarGridSpec(
        num_scalar_prefetch=1,
        grid=(e,),
        in_specs=[
            pl.BlockSpec((1, f, d), lambda i, ts: (i, 0, 0)),
            pl.BlockSpec(memory_space=pl.ANY),
        ],
        out_specs=pl.BlockSpec(memory_space=pl.ANY),
        scratch_shapes=[
            pltpu.VMEM((f, d), BF16),
            pltpu.VMEM((2, tm, f), BF16),
            pltpu.VMEM((2, tm, d // 2), U32),
            pltpu.VMEM((tm, d // 2), U32),
            pltpu.SemaphoreType.DMA((2,)),
            pltpu.SemaphoreType.DMA((2,)),
            pltpu.SemaphoreType.DMA(()),
        ],
    )
    return pl.pallas_call(
        _expert_down_kernel,
        grid_spec=grid_spec,
        out_shape=jax.ShapeDtypeStruct((p, d // 2), U32),
        compiler_params=_cparams(("arbitrary",)),
        name="expert_down",
    )(tile_start, w_down_e, act)


def _shared_up_kernel(h_ref, wg_ref, wu_ref, o_ref):
    h = h_ref[...]
    g = _dot(h, wg_ref[...].astype(BF16))
    u = _dot(h, wu_ref[...].astype(BF16))
    o_ref[...] = (g * jax.nn.sigmoid(g) * u).astype(o_ref.dtype)


def _shared_up(h, w_gate_s, w_up_s, bm=1024, bn=256):
    n, d = h.shape
    f = w_gate_s.shape[1]
    return pl.pallas_call(
        _shared_up_kernel,
        grid=(n // bm, f // bn),
        in_specs=[
            pl.BlockSpec((bm, d), lambda i, j: (i, 0)),
            pl.BlockSpec((d, bn), lambda i, j: (0, j)),
            pl.BlockSpec((d, bn), lambda i, j: (0, j)),
        ],
        out_specs=pl.BlockSpec((bm, bn), lambda i, j: (i, j)),
        out_shape=jax.ShapeDtypeStruct((n, f), BF16),
        compiler_params=_cparams(("arbitrary", "arbitrary")),
        name="shared_up",
    )(h, w_gate_s, w_up_s)


def _combine_kernel(pos_ref, pos_next_ref, x_ref, s_ref, wt_ref, mod_ref, g_ref, ys_ref, wd_ref, o_ref,
                    ybuf, wd_scr, stage, sem, wd_sem):
    i = pl.program_id(0)
    bt = x_ref.shape[0]
    slot = i % 2

    @pl.when(i == 0)
    def _():
        rows = stage.shape[0]
        for part in range(wd_scr.shape[0] // rows):
            cp = pltpu.make_async_copy(wd_ref.at[pl.ds(part * rows, rows)], stage, wd_sem)
            cp.start()
            cp.wait()
            wd_scr[part * rows:(part + 1) * rows, :] = stage[...].astype(BF16)

    def gather(p_ref, dst_slot):
        def issue(r, _):
            for k in range(TOP_K):
                p = p_ref[0, k, r]
                pltpu.make_async_copy(ys_ref.at[pl.ds(p, 1)], ybuf.at[dst_slot, k, pl.ds(r, 1)],
                                      sem.at[dst_slot]).start(priority=k % DMA_QUEUES)
            return 0

        lax.fori_loop(0, bt, issue, 0, unroll=ROW_DMA_UNROLL)

    @pl.when(i == 0)
    def _():
        gather(pos_ref, 0)

    @pl.when(i + 1 < pl.num_programs(0))
    def _():
        gather(pos_next_ref, 1 - slot)

    for k in range(TOP_K):
        pltpu.make_async_copy(ys_ref.at[pl.ds(0, bt)], ybuf.at[slot, k], sem.at[slot]).wait()

    c = ybuf.shape[3]
    acc_lo = jnp.zeros((bt, c), F32)
    acc_hi = jnp.zeros((bt, c), F32)
    for k in range(TOP_K):
        lo, hi = _unpack_halves(ybuf[slot, k])
        wk = wt_ref[:, k:k + 1]
        acc_lo = acc_lo + wk * lo
        acc_hi = acc_hi + wk * hi
    moe = jnp.concatenate([acc_lo, acc_hi], axis=1) + _dot(s_ref[...], wd_scr[...])
    gate = mod_ref[0][5:6]
    x = x_ref[...] + gate * moe
    y = x * lax.rsqrt(jnp.mean(x * x, axis=-1, keepdims=True) + EPS) * g_ref[...]
    o_ref[...] = y


def _combine(pos3, x1, sact, wts_t, mod, final_g, ys, w_down_s, seq, bt, stage_rows=128):
    n, d = x1.shape
    f = sact.shape[1]
    nb = n // bt
    pos_spec = lambda imap: pl.BlockSpec((1, K_PAD, bt), imap, memory_space=pltpu.SMEM)
    return pl.pallas_call(
        _combine_kernel,
        grid=(nb,),
        in_specs=[
            pos_spec(lambda i: (i, 0, 0)),
            pos_spec(lambda i: (jnp.minimum(i + 1, nb - 1), 0, 0)),
            pl.BlockSpec((bt, d), lambda i: (i, 0)),
            pl.BlockSpec((bt, f), lambda i: (i, 0)),
            pl.BlockSpec((bt, K_PAD), lambda i: (i, 0)),
            pl.BlockSpec((1, N_MOD, d), lambda i: (i * bt // seq, 0, 0)),
            pl.BlockSpec((1, d), lambda i: (0, 0)),
            pl.BlockSpec(memory_space=pl.ANY),
            pl.BlockSpec(memory_space=pl.ANY),
        ],
        out_specs=pl.BlockSpec((bt, d), lambda i: (i, 0)),
        out_shape=jax.ShapeDtypeStruct((n, d), F32),
        scratch_shapes=[
            pltpu.VMEM((2, TOP_K, bt, d // 2), U32),
            pltpu.VMEM((f, d), BF16),
            pltpu.VMEM((stage_rows, d), F32),
            pltpu.SemaphoreType.DMA((2,)),
            pltpu.SemaphoreType.DMA(()),
        ],
        compiler_params=_cparams(("arbitrary",)),
        name="combine",
    )(pos3, pos3, x1, sact, wts_t, mod, final_g.reshape(1, d), ys, w_down_s)


def _moe_tiles(counts, n_tokens, tm):
    e = counts.shape[0]
    n_tiles = n_tokens * TOP_K // tm + e
    tiles_per = (counts + tm - 1) // tm
    tile_end = jnp.cumsum(tiles_per)
    tile_start = jnp.concatenate([jnp.zeros((1,), I32), tile_end.astype(I32)])
    offsets = tile_start[:-1] * tm
    pads = jnp.stack([offsets + counts, tiles_per * tm - counts]).astype(I32)
    return offsets, tile_start, pads, n_tiles


def kernel(x, c, w_ada, b_ada, norm1_g, w_in, w_s, b_s, w_up_a, w_up_b, w_o, norm2_g, w_router,
           router_bias, w_gate_e, w_up_e, w_down_e, w_gate_s, w_up_s, w_down_s, final_g):
    batch, seq, d = x.shape
    n = batch * seq
    depth = w_ada.shape[0]
    a_width = w_up_a.shape[1]
    b_width = w_up_b.shape[1]
    n_heads = b_width // HEAD_DIM
    tm = 512
    bt_dispatch = 256
    bt_combine = 128

    assert depth == 1, "the combine kernel applies the final norm, so only one layer is supported"
    l = 0
    x2 = x.reshape(n, d)
    mod = _ada(c, w_ada[l], b_ada[l])
    h1 = _norm1(x2, norm1_g[l], mod, seq)
    proj = _inproj(h1, w_in[l], a_width, b_width)
    sgu = _sgu(proj, w_s[l], b_s[l], a_width)
    att = _attention(proj, batch, seq, 2 * a_width, 2 * a_width + b_width,
                     2 * a_width + 2 * b_width, n_heads)
    merged = _merge(sgu, att, w_up_a[l], w_up_b[l], proj,
                    2 * a_width + 3 * b_width, 2 * a_width + 3 * b_width + d)
    x1 = _resid_matmul(merged, w_o[l], x2, mod, 2, seq, "out_proj")

    h2, h2p, ids, wts, ranks, counts = _router(x1, norm2_g[l], mod, w_router[l], router_bias[l], seq)
    offsets, tile_start, pads, n_tiles = _moe_tiles(counts[:, 0], n, tm)
    pos = _positions(offsets, ids, ranks)
    blocked = lambda b: pos.reshape(K_PAD, n // b, b).transpose(1, 0, 2)
    xs = _dispatch(h2p, blocked(bt_dispatch), pads, n_tiles * tm, bt_dispatch, tm)
    act = _expert_up(tile_start, xs, w_gate_e[l], w_up_e[l], tm)
    ys = _expert_down(tile_start, act, w_down_e[l], tm)
    sact = _shared_up(h2, w_gate_s[l], w_up_s[l])
    out = _combine(blocked(bt_combine), x1, sact, wts.T, mod, final_g, ys, w_down_s[l], seq, bt_combine)
    return out.reshape(batch, seq, d)
```

```python
import functools

import jax
import jax.numpy as jnp
from jax import lax
from jax.experimental import pallas as pl
from jax.experimental.pallas import tpu as pltpu

F32 = jnp.float32
BF16 = jnp.bfloat16
I32 = jnp.int32
U32 = jnp.uint32

EPS = 1e-6
A_GROUPS = 8
A_CHUNK = 128
HEAD_DIM = 128
N_EXPERTS = 64
TOP_K = 6
N_GROUPS = 8
TOPK_GROUPS = 4
ROUTED_SCALE = 2.5
N_MOD = 6
SUBLANES = 8
K_PAD = SUBLANES
ROW_DMA_UNROLL = 8
DMA_QUEUES = 2
TILE_DMA_PRIORITY = 1
DEAD_LOG = -105.0

VMEM_LIMIT = 56 * 1024 * 1024


def _cparams(sem):
    return pltpu.CompilerParams(dimension_semantics=sem, vmem_limit_bytes=VMEM_LIMIT)


def _dot(a, b):
    return jnp.dot(a, b, preferred_element_type=F32)


def _dot_nt(a, b):
    return lax.dot_general(a, b, (((1,), (1,)), ((), ())), preferred_element_type=F32)


def _pack_halves(h):
    c = h.shape[1] // 2
    lo = pltpu.bitcast(h[:, :c].astype(BF16).astype(F32), U32)
    hi = pltpu.bitcast(h[:, c:].astype(BF16).astype(F32), U32)
    return (hi & jnp.uint32(0xFFFF0000)) | (lo >> 16)


def _unpack_halves(p):
    lo = pltpu.bitcast(p << 16, F32)
    hi = pltpu.bitcast(p & jnp.uint32(0xFFFF0000), F32)
    return lo, hi


def _ada_kernel(c_ref, w_ref, b_ref, o_ref):
    c = c_ref[...]
    cact = (c * jax.nn.sigmoid(c)).astype(BF16)
    o_ref[...] = _dot(cact, w_ref[...].astype(BF16)) + b_ref[...]


def _ada(c, w_ada, b_ada, bn=512):
    b, d = c.shape
    n = w_ada.shape[1]
    cp = jnp.zeros((8, d), F32).at[:b].set(c)
    out = pl.pallas_call(
        _ada_kernel,
        grid=(n // bn,),
        in_specs=[
            pl.BlockSpec((8, d), lambda j: (0, 0)),
            pl.BlockSpec((d, bn), lambda j: (0, j)),
            pl.BlockSpec((1, bn), lambda j: (0, j)),
        ],
        out_specs=pl.BlockSpec((8, bn), lambda j: (0, j)),
        out_shape=jax.ShapeDtypeStruct((8, n), F32),
        compiler_params=_cparams(("arbitrary",)),
        name="ada",
    )(cp, w_ada, b_ada.reshape(1, n))
    return out[:b].reshape(b, N_MOD, d)


def _norm_mod(x, g_ref, mod_ref, shift_idx, scale_idx):
    y = x * lax.rsqrt(jnp.mean(x * x, axis=-1, keepdims=True) + EPS) * g_ref[...]
    mod = mod_ref[0]
    return y * (1.0 + mod[scale_idx:scale_idx + 1]) + mod[shift_idx:shift_idx + 1]


def _norm1_kernel(x_ref, g_ref, mod_ref, o_ref):
    o_ref[...] = _norm_mod(x_ref[...], g_ref, mod_ref, 0, 1).astype(BF16)


def _norm1(x2, g, mod, seq, bm=512):
    n, d = x2.shape
    return pl.pallas_call(
        _norm1_kernel,
        grid=(n // bm,),
        in_specs=[
            pl.BlockSpec((bm, d), lambda i: (i, 0)),
            pl.BlockSpec((1, d), lambda i: (0, 0)),
            pl.BlockSpec((1, N_MOD, d), lambda i: (i * bm // seq, 0, 0)),
        ],
        out_specs=pl.BlockSpec((bm, d), lambda i: (i, 0)),
        out_shape=jax.ShapeDtypeStruct((n, d), BF16),
        compiler_params=_cparams(("arbitrary",)),
        name="norm1",
    )(x2, g.reshape(1, d), mod)


def _inproj_kernel(h_ref, w_ref, o_ref, *, n_gelu, n_ln, n_plain, group):
    j = pl.program_id(1)

    def run(epilogue):
        for s in range(o_ref.shape[1] // group):
            cs = slice(s * group, (s + 1) * group)
            acc = _dot(h_ref[...], w_ref[:, cs].astype(BF16))
            o_ref[:, cs] = epilogue(acc).astype(o_ref.dtype)

    def group_norm(acc):
        a = jax.nn.gelu(acc)
        cen = a - jnp.mean(a, axis=-1, keepdims=True)
        var = jnp.mean(cen * cen, axis=-1, keepdims=True)
        return cen * lax.rsqrt(var + EPS)

    @pl.when(j < n_gelu)
    def _():
        run(jax.nn.gelu)

    @pl.when((j >= n_gelu) & (j < n_ln))
    def _():
        run(group_norm)

    @pl.when((j >= n_ln) & (j < n_plain))
    def _():
        run(lambda acc: acc)

    @pl.when(j >= n_plain)
    def _():
        run(jax.nn.sigmoid)


def _inproj(h, w_in, a_width, b_width, bm=1024, bn=512):
    n, d = h.shape
    cols = w_in.shape[1]
    group = a_width // A_GROUPS
    kern = functools.partial(
        _inproj_kernel,
        n_gelu=a_width // bn,
        n_ln=2 * a_width // bn,
        n_plain=(2 * a_width + 3 * b_width) // bn,
        group=group,
    )
    return pl.pallas_call(
        kern,
        grid=(n // bm, cols // bn),
        in_specs=[
            pl.BlockSpec((bm, d), lambda i, j: (i, 0)),
            pl.BlockSpec((d, bn), lambda i, j: (0, j)),
        ],
        out_specs=pl.BlockSpec((bm, bn), lambda i, j: (i, j)),
        out_shape=jax.ShapeDtypeStruct((n, cols), BF16),
        compiler_params=_cparams(("arbitrary", "arbitrary")),
        name="inproj",
    )(h, w_in)


def _sgu_kernel(u_ref, v_ref, ws_ref, bs_ref, o_ref, *, group):
    t = A_CHUNK
    row = lax.broadcasted_iota(I32, (t, t), 0)
    col = lax.broadcasted_iota(I32, (t, t), 1)
    for g in range(A_GROUPS):
        ws = jnp.where(row >= col, ws_ref[g], 0.0).astype(BF16)
        bias = bs_ref[:, g:g + 1]
        for c in range(u_ref.shape[0] // t):
            rs = slice(c * t, (c + 1) * t)
            cs = slice(g * group, (g + 1) * group)
            mixed = _dot(ws, v_ref[rs, cs]) + bias
            o_ref[rs, cs] = (u_ref[rs, cs].astype(F32) * mixed).astype(o_ref.dtype)


def _sgu(proj, w_s, b_s, a_width, bm=512):
    n = proj.shape[0]
    group = a_width // A_GROUPS
    return pl.pallas_call(
        functools.partial(_sgu_kernel, group=group),
        grid=(n // bm,),
        in_specs=[
            pl.BlockSpec((bm, a_width), lambda i: (i, 0)),
            pl.BlockSpec((bm, a_width), lambda i: (i, 1)),
            pl.BlockSpec((A_GROUPS, A_CHUNK, A_CHUNK), lambda i: (0, 0, 0)),
            pl.BlockSpec((A_CHUNK, A_GROUPS), lambda i: (0, 0)),
        ],
        out_specs=pl.BlockSpec((bm, a_width), lambda i: (i, 0)),
        out_shape=jax.ShapeDtypeStruct((n, a_width), BF16),
        compiler_params=_cparams(("arbitrary",)),
        name="sgu",
    )(proj, proj, w_s, b_s.T)


def _attn_kernel(q_ref, k_ref, v_ref, o_ref, *, blk, heads, scale):
    seq = q_ref.shape[0]
    row = lax.broadcasted_iota(I32, (blk, blk), 0)
    col = lax.broadcasted_iota(I32, (blk, blk), 1)
    upper = jnp.where(row > col, 1.0, 0.0).astype(BF16)
    strict = col < row

    hslices = [slice(h * HEAD_DIM, (h + 1) * HEAD_DIM) for h in range(heads)]

    def update(qv, ks, cs, accs, diagonal):
        zs = [_dot_nt(qv[h], k_ref[pl.ds(ks, blk), hslices[h]]) * scale for h in range(heads)]
        sps = [jnp.maximum(z, 0.0) + jnp.log(1.0 + jnp.exp(-jnp.abs(z))) for z in zs]
        stays = [jnp.where(strict, -sp, 0.0) if diagonal else -sp for sp in sps]
        his = [s.astype(BF16) for s in stays]
        los = [(s - hi.astype(F32)).astype(BF16) for s, hi in zip(stays, his)]
        laters = [_dot(hi, upper) + _dot(lo, upper) for hi, lo in zip(his, los)]
        ws = [jnp.exp((zs[h] - sps[h]) + laters[h] + cs[h]) for h in range(heads)]
        if diagonal:
            ws = [jnp.where(strict, w, 0.0) for w in ws]
        accs = tuple(accs[h] + _dot(ws[h].astype(BF16), v_ref[pl.ds(ks, blk), hslices[h]])
                     for h in range(heads))
        cs = tuple(cs[h] + laters[h][:, :1] + stays[h][:, :1] for h in range(heads))
        return cs, accs

    def q_block(qi, _):
        qs = pl.multiple_of(qi * blk, blk)
        qv = [q_ref[pl.ds(qs, blk), hs] for hs in hslices]
        cs, accs = update(qv, qs, (jnp.zeros((blk, 1), F32),) * heads,
                          (jnp.zeros((blk, HEAD_DIM), F32),) * heads, True)

        def live(carry):
            n, cs, _ = carry
            cmax = jnp.max(cs[0])
            for c in cs[1:]:
                cmax = jnp.maximum(cmax, jnp.max(c))
            return (n <= qi) & (cmax > DEAD_LOG)

        def older_block(carry):
            n, cs, accs = carry
            ks = pl.multiple_of((qi - n) * blk, blk)
            cs, accs = update(qv, ks, cs, accs, False)
            return n + 1, cs, accs

        _, _, accs = lax.while_loop(live, older_block, (jnp.int32(1), cs, accs))
        for h in range(heads):
            o_ref[pl.ds(qs, blk), hslices[h]] = accs[h].astype(o_ref.dtype)
        return 0

    lax.fori_loop(0, seq // blk, q_block, 0)


def _attention(proj, batch, seq, q_col, k_col, v_col, n_heads, blk=256, heads=4):
    width = heads * HEAD_DIM
    qc, kc, vc = q_col // width, k_col // width, v_col // width
    return pl.pallas_call(
        functools.partial(_attn_kernel, blk=blk, heads=heads, scale=HEAD_DIM ** -0.5),
        grid=(batch, n_heads // heads),
        in_specs=[
            pl.BlockSpec((seq, width), lambda b, h: (b, qc + h)),
            pl.BlockSpec((seq, width), lambda b, h: (b, kc + h)),
            pl.BlockSpec((seq, width), lambda b, h: (b, vc + h)),
        ],
        out_specs=pl.BlockSpec((seq, width), lambda b, h: (b, h)),
        out_shape=jax.ShapeDtypeStruct((batch * seq, n_heads * HEAD_DIM), BF16),
        compiler_params=_cparams(("arbitrary", "arbitrary")),
        name="attn",
    )(proj, proj, proj)


def _merge_kernel(a_ref, b_ref, wa_ref, wb_ref, ga_ref, gb_ref, o_ref):
    ya = _dot(a_ref[...], wa_ref[...].astype(BF16))
    yb = _dot(b_ref[...], wb_ref[...].astype(BF16))
    o_ref[...] = (ga_ref[...].astype(F32) * ya + gb_ref[...].astype(F32) * yb).astype(o_ref.dtype)


def _merge(sgu, att, w_up_a, w_up_b, proj, ga_col, gb_col, bm=1024, bn=512):
    n, ka = sgu.shape
    kb = att.shape[1]
    d = w_up_a.shape[1]
    gac, gbc = ga_col // bn, gb_col // bn
    return pl.pallas_call(
        _merge_kernel,
        grid=(n // bm, d // bn),
        in_specs=[
            pl.BlockSpec((bm, ka), lambda i, j: (i, 0)),
            pl.BlockSpec((bm, kb), lambda i, j: (i, 0)),
            pl.BlockSpec((ka, bn), lambda i, j: (0, j)),
            pl.BlockSpec((kb, bn), lambda i, j: (0, j)),
            pl.BlockSpec((bm, bn), lambda i, j: (i, gac + j)),
            pl.BlockSpec((bm, bn), lambda i, j: (i, gbc + j)),
        ],
        out_specs=pl.BlockSpec((bm, bn), lambda i, j: (i, j)),
        out_shape=jax.ShapeDtypeStruct((n, d), BF16),
        compiler_params=_cparams(("arbitrary", "arbitrary")),
        name="merge",
    )(sgu, att, w_up_a, w_up_b, proj, proj)


def _resid_kernel(a_ref, w_ref, x_ref, mod_ref, o_ref, *, gate_idx):
    y = _dot(a_ref[...], w_ref[...].astype(BF16))
    gate = mod_ref[0][gate_idx:gate_idx + 1]
    o_ref[...] = x_ref[...] + gate * y


def _resid_matmul(a, w, x2, mod, gate_idx, seq, name, bm=1024, bn=512):
    n, k = a.shape
    d = w.shape[1]
    return pl.pallas_call(
        functools.partial(_resid_kernel, gate_idx=gate_idx),
        grid=(n // bm, d // bn),
        in_specs=[
            pl.BlockSpec((bm, k), lambda i, j: (i, 0)),
            pl.BlockSpec((k, bn), lambda i, j: (0, j)),
            pl.BlockSpec((bm, bn), lambda i, j: (i, j)),
            pl.BlockSpec((1, N_MOD, bn), lambda i, j: (i * bm // seq, 0, j)),
        ],
        out_specs=pl.BlockSpec((bm, bn), lambda i, j: (i, j)),
        out_shape=jax.ShapeDtypeStruct((n, d), F32),
        compiler_params=_cparams(("arbitrary", "arbitrary")),
        name=name,
    )(a, w, x2, mod)


def _first_argmax(vals, iota, size):
    m = jnp.max(vals, axis=0, keepdims=True)
    idx = jnp.min(jnp.where(vals == m, iota, size), axis=0, keepdims=True)
    return m, idx


def _router_kernel(x_ref, g_ref, mod_ref, wr_ref, rb_ref, h_ref, hp_ref, ids_ref, wts_ref, rank_ref,
                   cnt_ref, carry_ref):
    i = pl.program_id(0)
    bm = x_ref.shape[0]
    e = N_EXPERTS
    per = e // N_GROUPS

    @pl.when(i == 0)
    def _():
        carry_ref[...] = jnp.zeros_like(carry_ref)

    h = _norm_mod(x_ref[...], g_ref, mod_ref, 3, 4)
    h_ref[...] = h.astype(BF16)
    hp_ref[...] = _pack_halves(h)

    h_hi = h.astype(BF16)
    h_lo = (h - h_hi.astype(F32)).astype(BF16)
    wr = wr_ref[...]
    w_hi = wr.astype(BF16)
    w_lo = (wr - w_hi.astype(F32)).astype(BF16)
    logits = _dot_nt(w_hi, h_hi) + (_dot_nt(w_hi, h_lo) + _dot_nt(w_lo, h_hi))
    scores = jax.nn.sigmoid(logits)
    choice = scores + rb_ref[...]

    iota_p = lax.broadcasted_iota(I32, (per, bm), 0)
    gs_rows = []
    for g in range(N_GROUPS):
        cg = choice[g * per:(g + 1) * per, :]
        m1, i1 = _first_argmax(cg, iota_p, per)
        m2 = jnp.max(jnp.where(iota_p == i1, -jnp.inf, cg), axis=0, keepdims=True)
        gs_rows.append(m1 + m2)
    gs = jnp.concatenate(gs_rows, axis=0)

    iota_g = lax.broadcasted_iota(I32, (N_GROUPS, bm), 0)
    gsel = jnp.zeros((N_GROUPS, bm), F32)
    for _ in range(TOPK_GROUPS):
        _, ig = _first_argmax(gs, iota_g, N_GROUPS)
        hit = iota_g == ig
        gsel = jnp.where(hit, 1.0, gsel)
        gs = jnp.where(hit, -jnp.inf, gs)
    emask = jnp.concatenate(
        [jnp.broadcast_to(gsel[g:g + 1, :], (per, bm)) for g in range(N_GROUPS)], axis=0)

    iota_e = lax.broadcasted_iota(I32, (e, bm), 0)
    masked = jnp.where(emask > 0.5, choice, -jnp.inf)
    self_f = jnp.zeros((e, bm), F32)
    idx_rows, w_rows = [], []
    for _ in range(TOP_K):
        _, ie = _first_argmax(masked, iota_e, e)
        hit = iota_e == ie
        self_f = jnp.where(hit, 1.0, self_f)
        idx_rows.append(ie)
        w_rows.append(jnp.sum(jnp.where(hit, scores, 0.0), axis=0, keepdims=True))
        masked = jnp.where(hit, -jnp.inf, masked)
    w = jnp.concatenate(w_rows, axis=0)
    w = w / jnp.sum(w, axis=0, keepdims=True) * ROUTED_SCALE

    t_row = lax.broadcasted_iota(I32, (bm, bm), 0)
    t_col = lax.broadcasted_iota(I32, (bm, bm), 1)
    before = jnp.where(t_row < t_col, 1.0, 0.0).astype(BF16)
    rank_mat = _dot(self_f.astype(BF16), before) + carry_ref[:, :1]
    rank_rows = [jnp.sum(jnp.where(iota_e == ie, rank_mat, 0.0), axis=0, keepdims=True)
                 for ie in idx_rows]
    carry_ref[...] = carry_ref[...] + jnp.sum(self_f, axis=1, keepdims=True)

    pad_i = jnp.zeros((K_PAD - TOP_K, bm), I32)
    ids_ref[...] = jnp.concatenate(idx_rows + [pad_i], axis=0)
    wts_ref[...] = jnp.concatenate([w, jnp.zeros((K_PAD - TOP_K, bm), F32)], axis=0)
    rank_ref[...] = jnp.concatenate([r.astype(I32) for r in rank_rows] + [pad_i], axis=0)
    cnt_ref[...] = carry_ref[...].astype(I32)


def _router(x1, g, mod, w_router, router_bias, seq, bm=512):
    n, d = x1.shape
    e = N_EXPERTS
    tok = lambda i: (0, i)
    return pl.pallas_call(
        _router_kernel,
        grid=(n // bm,),
        in_specs=[
            pl.BlockSpec((bm, d), lambda i: (i, 0)),
            pl.BlockSpec((1, d), lambda i: (0, 0)),
            pl.BlockSpec((1, N_MOD, d), lambda i: (i * bm // seq, 0, 0)),
            pl.BlockSpec((e, d), lambda i: (0, 0)),
            pl.BlockSpec((e, 1), lambda i: (0, 0)),
        ],
        out_specs=[
            pl.BlockSpec((bm, d), lambda i: (i, 0)),
            pl.BlockSpec((bm, d // 2), lambda i: (i, 0)),
            pl.BlockSpec((K_PAD, bm), tok),
            pl.BlockSpec((K_PAD, bm), tok),
            pl.BlockSpec((K_PAD, bm), tok),
            pl.BlockSpec((e, 128), lambda i: (0, 0)),
        ],
        out_shape=[
            jax.ShapeDtypeStruct((n, d), BF16),
            jax.ShapeDtypeStruct((n, d // 2), U32),
            jax.ShapeDtypeStruct((K_PAD, n), I32),
            jax.ShapeDtypeStruct((K_PAD, n), F32),
            jax.ShapeDtypeStruct((K_PAD, n), I32),
            jax.ShapeDtypeStruct((e, 128), I32),
        ],
        scratch_shapes=[pltpu.VMEM((e, 128), F32)],
        compiler_params=_cparams(("arbitrary",)),
        name="router",
    )(x1, g.reshape(1, d), mod, w_router.T, router_bias.reshape(e, 1))


def _pos_kernel(off_ref, ids_ref, rank_ref, o_ref):
    ids = ids_ref[...]
    pos = rank_ref[...]
    for ex in range(N_EXPERTS):
        pos = pos + jnp.where(ids == ex, off_ref[ex], 0)
    o_ref[...] = pos


def _positions(offsets, ids, ranks):
    return pl.pallas_call(
        _pos_kernel,
        in_specs=[
            pl.BlockSpec(memory_space=pltpu.SMEM),
            pl.BlockSpec(memory_space=pltpu.VMEM),
            pl.BlockSpec(memory_space=pltpu.VMEM),
        ],
        out_specs=pl.BlockSpec(memory_space=pltpu.VMEM),
        out_shape=jax.ShapeDtypeStruct(ids.shape, I32),
        name="positions",
    )(offsets, ids, ranks)


def _dispatch_kernel(pad_ref, pos_ref, h_ref, xs_ref, zeros, sem, pad_sem, *, tm):
    bt = h_ref.shape[0]

    @pl.when(pl.program_id(0) == 0)
    def _():
        zeros[...] = jnp.zeros_like(zeros)
        group_bits = [1 << b for b in reversed(range((tm // SUBLANES - 1).bit_length()))]

        def pad_copies(ex):
            start, gap = pad_ref[0, ex], pad_ref[1, ex]
            end = start + gap
            groups, singles = gap >> 3, gap & (SUBLANES - 1)
            out = []
            for j in range(SUBLANES - 1):
                out.append((j < singles, pltpu.make_async_copy(
                    zeros.at[pl.ds(0, 1)], xs_ref.at[pl.ds(start + j, 1)], pad_sem)))
            for bit in group_bits:
                first = pl.multiple_of(end - SUBLANES * (groups & (2 * bit - 1)), SUBLANES)
                out.append(((groups & bit) != 0, pltpu.make_async_copy(
                    zeros.at[pl.ds(0, SUBLANES * bit)], xs_ref.at[pl.ds(first, SUBLANES * bit)], pad_sem)))
            return out

        def issue_pad(ex, _):
            for cond, cp in pad_copies(ex):
                @pl.when(cond)
                def _():
                    cp.start()
            return 0

        def drain_pad(ex, _):
            for cond, cp in pad_copies(ex):
                @pl.when(cond)
                def _():
                    cp.wait()
            return 0

        lax.fori_loop(0, N_EXPERTS, issue_pad, 0)
        lax.fori_loop(0, N_EXPERTS, drain_pad, 0)

        zrows = zeros.shape[0]
        first = (pad_ref[0, N_EXPERTS - 1] + pad_ref[1, N_EXPERTS - 1]) // zrows

        def tail_copy(g):
            return pltpu.make_async_copy(zeros, xs_ref.at[pl.ds(pl.multiple_of(g * zrows, zrows), zrows)],
                                         pad_sem)

        def issue_tail(g, _):
            tail_copy(g).start()
            return 0

        def drain_tail(g, _):
            tail_copy(g).wait()
            return 0

        lax.fori_loop(first, xs_ref.shape[0] // zrows, issue_tail, 0)
        lax.fori_loop(first, xs_ref.shape[0] // zrows, drain_tail, 0)

    def issue(r, _):
        for k in range(TOP_K):
            p = pos_ref[0, k, r]
            pltpu.make_async_copy(h_ref.at[pl.ds(r, 1)], xs_ref.at[pl.ds(p, 1)], sem).start(
                priority=k % DMA_QUEUES)
        return 0

    lax.fori_loop(0, bt, issue, 0, unroll=ROW_DMA_UNROLL)
    for _ in range(TOP_K):
        pltpu.make_async_copy(h_ref, xs_ref.at[pl.ds(0, bt)], sem).wait()


def _dispatch(hp, pos3, pads, n_rows, bt, tm):
    n, c = hp.shape
    return pl.pallas_call(
        functools.partial(_dispatch_kernel, tm=tm),
        grid=(n // bt,),
        in_specs=[
            pl.BlockSpec(memory_space=pltpu.SMEM),
            pl.BlockSpec((1, K_PAD, bt), lambda i: (i, 0, 0), memory_space=pltpu.SMEM),
            pl.BlockSpec((bt, c), lambda i: (i, 0)),
        ],
        out_specs=pl.BlockSpec(memory_space=pl.ANY),
        out_shape=jax.ShapeDtypeStruct((n_rows, c), U32),
        scratch_shapes=[pltpu.VMEM((tm // 2, c), U32), pltpu.SemaphoreType.DMA(()),
                        pltpu.SemaphoreType.DMA(())],
        compiler_params=_cparams(("arbitrary",)),
        name="dispatch",
    )(pads, pos3, hp)


def _expert_tiles(ts_ref, src_ref, dst_ref, in_buf, out_buf, zero_buf, in_sem, out_sem, zero_sem, compute):
    e = pl.program_id(0)
    n_exp = pl.num_programs(0)
    n_in, tm = in_buf.shape[0], in_buf.shape[1]
    n_out = out_buf.shape[0]
    n_tiles = dst_ref.shape[0] // tm
    n_used = ts_ref[n_exp]

    def rows(g):
        return pl.ds(pl.multiple_of(g * tm, tm), tm)

    def load(g):
        return pltpu.make_async_copy(src_ref.at[rows(g)], in_buf.at[g % n_in], in_sem.at[g % n_in])

    def store(g):
        return pltpu.make_async_copy(out_buf.at[g % n_out], dst_ref.at[rows(g)], out_sem.at[g % n_out])

    @pl.when(e == 0)
    def _():
        for g in range(n_in - 1):
            @pl.when(g < n_used)
            def _():
                load(g).start(priority=TILE_DMA_PRIORITY)

    def tile(g, _):
        @pl.when(g + n_in - 1 < n_used)
        def _():
            load(g + n_in - 1).start(priority=TILE_DMA_PRIORITY)

        load(g).wait()

        @pl.when(g >= n_out)
        def _():
            store(g - n_out).wait()

        out_buf[g % n_out] = compute(in_buf[g % n_in])
        store(g).start(priority=TILE_DMA_PRIORITY)
        return 0

    lax.fori_loop(ts_ref[e], ts_ref[e + 1], tile, 0)

    @pl.when(e == n_exp - 1)
    def _():
        for back in range(n_out, 0, -1):
            @pl.when(n_used >= back)
            def _():
                store(n_used - back).wait()

        zero_buf[...] = jnp.zeros_like(zero_buf)

        def fill(g):
            return pltpu.make_async_copy(zero_buf, dst_ref.at[rows(g)], zero_sem)

        def start_fill(g, _):
            fill(g).start()
            return 0

        def wait_fill(g, _):
            fill(g).wait()
            return 0

        lax.fori_loop(n_used, n_tiles, start_fill, 0)
        lax.fori_loop(n_used, n_tiles, wait_fill, 0)


def _expert_up_kernel(ts_ref, wg_ref, wu_ref, xs_ref, act_ref, w_scr, in_buf, out_buf, zero_buf,
                      in_sem, out_sem, zero_sem):
    e = pl.program_id(0)
    f = wg_ref.shape[2]

    @pl.when(ts_ref[e + 1] > ts_ref[e])
    def _():
        w_scr[:, :f] = wg_ref[0].astype(BF16)
        w_scr[:, f:] = wu_ref[0].astype(BF16)

    def compute(xp):
        lo, hi = _unpack_halves(xp)
        x = jnp.concatenate([lo.astype(BF16), hi.astype(BF16)], axis=1)
        gu = _dot(x, w_scr[...])
        g, u = gu[:, :f], gu[:, f:]
        return (g * jax.nn.sigmoid(g) * u).astype(BF16)

    _expert_tiles(ts_ref, xs_ref, act_ref, in_buf, out_buf, zero_buf, in_sem, out_sem, zero_sem, compute)


def _expert_up(tile_start, xs, w_gate_e, w_up_e, tm, in_slots=4, out_slots=8):
    p, c = xs.shape
    e, d, f = w_gate_e.shape
    grid_spec = pltpu.PrefetchScalarGridSpec(
        num_scalar_prefetch=1,
        grid=(e,),
        in_specs=[
            pl.BlockSpec((1, d, f), lambda i, ts: (i, 0, 0)),
            pl.BlockSpec((1, d, f), lambda i, ts: (i, 0, 0)),
            pl.BlockSpec(memory_space=pl.ANY),
        ],
        out_specs=pl.BlockSpec(memory_space=pl.ANY),
        scratch_shapes=[
            pltpu.VMEM((d, 2 * f), BF16),
            pltpu.VMEM((in_slots, tm, c), U32),
            pltpu.VMEM((out_slots, tm, f), BF16),
            pltpu.VMEM((tm, f), BF16),
            pltpu.SemaphoreType.DMA((in_slots,)),
            pltpu.SemaphoreType.DMA((out_slots,)),
            pltpu.SemaphoreType.DMA(()),
        ],
    )
    return pl.pallas_call(
        _expert_up_kernel,
        grid_spec=grid_spec,
        out_shape=jax.ShapeDtypeStruct((p, f), BF16),
        compiler_params=_cparams(("arbitrary",)),
        name="expert_up",
    )(tile_start, w_gate_e, w_up_e, xs)


def _expert_down_kernel(ts_ref, wd_ref, act_ref, ys_ref, w_scr, in_buf, out_buf, zero_buf,
                        in_sem, out_sem, zero_sem):
    e = pl.program_id(0)

    @pl.when(ts_ref[e + 1] > ts_ref[e])
    def _():
        w_scr[...] = wd_ref[0].astype(BF16)

    def compute(a):
        return _pack_halves(_dot(a, w_scr[...]))

    _expert_tiles(ts_ref, act_ref, ys_ref, in_buf, out_buf, zero_buf, in_sem, out_sem, zero_sem, compute)


def _expert_down(tile_start, act, w_down_e, tm, in_slots=8, out_slots=8):
    p, f = act.shape
    e, _, d = w_down_e.shape
    grid_spec = pltpu.PrefetchScalarGridSpec(
        num_scalar_prefetch=1,
        grid=(e,),
        in_specs=[
            pl.BlockSpec((1, f, d), lambda i, ts: (i, 0, 0)),
            pl.BlockSpec(memory_space=pl.ANY),
        ],
        out_specs=pl.BlockSpec(memory_space=pl.ANY),
        scratch_shapes=[
            pltpu.VMEM((f, d), BF16),
            pltpu.VMEM((in_slots, tm, f), BF16),
            pltpu.VMEM((out_slots, tm, d // 2), U32),
            pltpu.VMEM((tm, d // 2), U32),
            pltpu.SemaphoreType.DMA((in_slots,)),
            pltpu.SemaphoreType.DMA((out_slots,)),
            pltpu.SemaphoreType.DMA(()),
        ],
    )
    return pl.pallas_call(
        _expert_down_kernel,
        grid_spec=grid_spec,
        out_shape=jax.ShapeDtypeStruct((p, d // 2), U32),
        compiler_params=_cparams(("arbitrary",)),
        name="expert_down",
    )(tile_start, w_down_e, act)


def _shared_up_kernel(h_ref, wg_ref, wu_ref, o_ref):
    h = h_ref[...]
    g = _dot(h, wg_ref[...].astype(BF16))
    u = _dot(h, wu_ref[...].astype(BF16))
    o_ref[...] = (g * jax.nn.sigmoid(g) * u).astype(o_ref.dtype)


def _shared_up(h, w_gate_s, w_up_s, bm=1024, bn=256):
    n, d = h.shape
    f = w_gate_s.shape[1]
    return pl.pallas_call(
        _shared_up_kernel,
        grid=(n // bm, f // bn),
        in_specs=[
            pl.BlockSpec((bm, d), lambda i, j: (i, 0)),
            pl.BlockSpec((d, bn), lambda i, j: (0, j)),
            pl.BlockSpec((d, bn), lambda i, j: (0, j)),
        ],
        out_specs=pl.BlockSpec((bm, bn), lambda i, j: (i, j)),
        out_shape=jax.ShapeDtypeStruct((n, f), BF16),
        compiler_params=_cparams(("arbitrary", "arbitrary")),
        name="shared_up",
    )(h, w_gate_s, w_up_s)


def _combine_kernel(pos_ref, pos_next_ref, x_ref, s_ref, wt_ref, mod_ref, g_ref, ys_ref, wd_ref, o_ref,
                    ybuf, wd_scr, stage, sem, wd_sem):
    i = pl.program_id(0)
    bt = x_ref.shape[0]
    slot = i % 2

    @pl.when(i == 0)
    def _():
        rows = stage.shape[0]
        for part in range(wd_scr.shape[0] // rows):
            cp = pltpu.make_async_copy(wd_ref.at[pl.ds(part * rows, rows)], stage, wd_sem)
            cp.start()
            cp.wait()
            wd_scr[part * rows:(part + 1) * rows, :] = stage[...].astype(BF16)

    def gather(p_ref, dst_slot):
        def issue(r, _):
            for k in range(TOP_K):
                p = p_ref[0, k, r]
                pltpu.make_async_copy(ys_ref.at[pl.ds(p, 1)], ybuf.at[dst_slot, pl.ds(k * bt + r, 1)],
                                      sem.at[dst_slot]).start(priority=k % DMA_QUEUES)
            return 0

        lax.fori_loop(0, bt, issue, 0, unroll=ROW_DMA_UNROLL)

    @pl.when(i == 0)
    def _():
        gather(pos_ref, 0)

    @pl.when(i + 1 < pl.num_programs(0))
    def _():
        gather(pos_next_ref, 1 - slot)

    pltpu.make_async_copy(ys_ref.at[pl.ds(0, TOP_K * bt)], ybuf.at[slot], sem.at[slot]).wait()

    c = ybuf.shape[2]
    acc_lo = jnp.zeros((bt, c), F32)
    acc_hi = jnp.zeros((bt, c), F32)
    for k in range(TOP_K):
        lo, hi = _unpack_halves(ybuf[slot, k * bt:(k + 1) * bt])
        wk = wt_ref[:, k:k + 1]
        acc_lo = acc_lo + wk * lo
        acc_hi = acc_hi + wk * hi
    moe = jnp.concatenate([acc_lo, acc_hi], axis=1) + _dot(s_ref[...], wd_scr[...])
    gate = mod_ref[0][5:6]
    x = x_ref[...] + gate * moe
    y = x * lax.rsqrt(jnp.mean(x * x, axis=-1, keepdims=True) + EPS) * g_ref[...]
    o_ref[...] = y


def _combine(pos3, x1, sact, wts_t, mod, final_g, ys, w_down_s, seq, bt, stage_rows=128):
    n, d = x1.shape
    f = sact.shape[1]
    nb = n // bt
    pos_spec = lambda imap: pl.BlockSpec((1, K_PAD, bt), imap, memory_space=pltpu.SMEM)
    return pl.pallas_call(
        _combine_kernel,
        grid=(nb,),
        in_specs=[
            pos_spec(lambda i: (i, 0, 0)),
            pos_spec(lambda i: (jnp.minimum(i + 1, nb - 1), 0, 0)),
            pl.BlockSpec((bt, d), lambda i: (i, 0)),
            pl.BlockSpec((bt, f), lambda i: (i, 0)),
            pl.BlockSpec((bt, K_PAD), lambda i: (i, 0)),
            pl.BlockSpec((1, N_MOD, d), lambda i: (i * bt // seq, 0, 0)),
            pl.BlockSpec((1, d), lambda i: (0, 0)),
            pl.BlockSpec(memory_space=pl.ANY),
            pl.BlockSpec(memory_space=pl.ANY),
        ],
        out_specs=pl.BlockSpec((bt, d), lambda i: (i, 0)),
        out_shape=jax.ShapeDtypeStruct((n, d), F32),
        scratch_shapes=[
            pltpu.VMEM((2, TOP_K * bt, d // 2), U32),
            pltpu.VMEM((f, d), BF16),
            pltpu.VMEM((stage_rows, d), F32),
            pltpu.SemaphoreType.DMA((2,)),
            pltpu.SemaphoreType.DMA(()),
        ],
        compiler_params=_cparams(("arbitrary",)),
        name="combine",
    )(pos3, pos3, x1, sact, wts_t, mod, final_g.reshape(1, d), ys, w_down_s)


def _moe_tiles(counts, n_tokens, tm):
    e = counts.shape[0]
    n_tiles = n_tokens * TOP_K // tm + e
    tiles_per = (counts + tm - 1) // tm
    tile_end = jnp.cumsum(tiles_per)
    tile_start = jnp.concatenate([jnp.zeros((1,), I32), tile_end.astype(I32)])
    offsets = tile_start[:-1] * tm
    pads = jnp.stack([offsets + counts, tiles_per * tm - counts]).astype(I32)
    return offsets, tile_start, pads, n_tiles


def kernel(x, c, w_ada, b_ada, norm1_g, w_in, w_s, b_s, w_up_a, w_up_b, w_o, norm2_g, w_router,
           router_bias, w_gate_e, w_up_e, w_down_e, w_gate_s, w_up_s, w_down_s, final_g):
    batch, seq, d = x.shape
    n = batch * seq
    depth = w_ada.shape[0]
    a_width = w_up_a.shape[1]
    b_width = w_up_b.shape[1]
    n_heads = b_width // HEAD_DIM
    tm = 256
    bt_dispatch = 512
    bt_combine = 128

    assert depth == 1, "the combine kernel applies the final norm, so only one layer is supported"
    l = 0
    x2 = x.reshape(n, d)
    mod = _ada(c, w_ada[l], b_ada[l])
    h1 = _norm1(x2, norm1_g[l], mod, seq)
    proj = _inproj(h1, w_in[l], a_width, b_width)
    sgu = _sgu(proj, w_s[l], b_s[l], a_width)
    att = _attention(proj, batch, seq, 2 * a_width, 2 * a_width + b_width,
                     2 * a_width + 2 * b_width, n_heads)
    merged = _merge(sgu, att, w_up_a[l], w_up_b[l], proj,
                    2 * a_width + 3 * b_width, 2 * a_width + 3 * b_width + d)
    x1 = _resid_matmul(merged, w_o[l], x2, mod, 2, seq, "out_proj")

    h2, h2p, ids, wts, ranks, counts = _router(x1, norm2_g[l], mod, w_router[l], router_bias[l], seq)
    offsets, tile_start, pads, n_tiles = _moe_tiles(counts[:, 0], n, tm)
    pos = _positions(offsets, ids, ranks)
    blocked = lambda b: pos.reshape(K_PAD, n // b, b).transpose(1, 0, 2)
    xs = _dispatch(h2p, blocked(bt_dispatch), pads, n_tiles * tm, bt_dispatch, tm)
    act = _expert_up(tile_start, xs, w_gate_e[l], w_up_e[l], tm)
    ys = _expert_down(tile_start, act, w_down_e[l], tm)
    sact = _shared_up(h2, w_gate_s[l], w_up_s[l])
    out = _combine(blocked(bt_combine), x1, sact, wts.T, mod, final_g, ys, w_down_s[l], seq, bt_combine)
    return out.reshape(batch, seq, d)
```

```python
import functools

import jax
import jax.numpy as jnp
from jax import lax
from jax.experimental import pallas as pl
from jax.experimental.pallas import tpu as pltpu

F32 = jnp.float32
BF16 = jnp.bfloat16
I32 = jnp.int32
U32 = jnp.uint32

EPS = 1e-6
A_GROUPS = 8
A_CHUNK = 128
HEAD_DIM = 128
N_EXPERTS = 64
TOP_K = 6
N_GROUPS = 8
TOPK_GROUPS = 4
ROUTED_SCALE = 2.5
N_MOD = 6
SUBLANES = 8
K_PAD = SUBLANES
ROW_DMA_UNROLL = 8
DMA_QUEUES = 2
TILE_DMA_PRIORITY = 1
DEAD_LOG = -105.0

VMEM_LIMIT = 56 * 1024 * 1024


def _cparams(sem):
    return pltpu.CompilerParams(dimension_semantics=sem, vmem_limit_bytes=VMEM_LIMIT)


def _dot(a, b):
    return jnp.dot(a, b, preferred_element_type=F32)


def _dot_nt(a, b):
    return lax.dot_general(a, b, (((1,), (1,)), ((), ())), preferred_element_type=F32)


def _pack_halves(h):
    c = h.shape[1] // 2
    lo = pltpu.bitcast(h[:, :c].astype(BF16).astype(F32), U32)
    hi = pltpu.bitcast(h[:, c:].astype(BF16).astype(F32), U32)
    return (hi & jnp.uint32(0xFFFF0000)) | (lo >> 16)


def _unpack_halves(p):
    lo = pltpu.bitcast(p << 16, F32)
    hi = pltpu.bitcast(p & jnp.uint32(0xFFFF0000), F32)
    return lo, hi


def _ada_kernel(c_ref, w_ref, b_ref, o_ref):
    c = c_ref[...]
    cact = (c * jax.nn.sigmoid(c)).astype(BF16)
    o_ref[...] = _dot(cact, w_ref[...].astype(BF16)) + b_ref[...]


def _ada(c, w_ada, b_ada, bn=512):
    b, d = c.shape
    n = w_ada.shape[1]
    cp = jnp.zeros((8, d), F32).at[:b].set(c)
    out = pl.pallas_call(
        _ada_kernel,
        grid=(n // bn,),
        in_specs=[
            pl.BlockSpec((8, d), lambda j: (0, 0)),
            pl.BlockSpec((d, bn), lambda j: (0, j)),
            pl.BlockSpec((1, bn), lambda j: (0, j)),
        ],
        out_specs=pl.BlockSpec((8, bn), lambda j: (0, j)),
        out_shape=jax.ShapeDtypeStruct((8, n), F32),
        compiler_params=_cparams(("arbitrary",)),
        name="ada",
    )(cp, w_ada, b_ada.reshape(1, n))
    return out[:b].reshape(b, N_MOD, d)


def _norm_mod(x, g_ref, mod_ref, shift_idx, scale_idx):
    y = x * lax.rsqrt(jnp.mean(x * x, axis=-1, keepdims=True) + EPS) * g_ref[...]
    mod = mod_ref[0]
    return y * (1.0 + mod[scale_idx:scale_idx + 1]) + mod[shift_idx:shift_idx + 1]


def _norm1_kernel(x_ref, g_ref, mod_ref, o_ref):
    o_ref[...] = _norm_mod(x_ref[...], g_ref, mod_ref, 0, 1).astype(BF16)


def _norm1(x2, g, mod, seq, bm=512):
    n, d = x2.shape
    return pl.pallas_call(
        _norm1_kernel,
        grid=(n // bm,),
        in_specs=[
            pl.BlockSpec((bm, d), lambda i: (i, 0)),
            pl.BlockSpec((1, d), lambda i: (0, 0)),
            pl.BlockSpec((1, N_MOD, d), lambda i: (i * bm // seq, 0, 0)),
        ],
        out_specs=pl.BlockSpec((bm, d), lambda i: (i, 0)),
        out_shape=jax.ShapeDtypeStruct((n, d), BF16),
        compiler_params=_cparams(("arbitrary",)),
        name="norm1",
    )(x2, g.reshape(1, d), mod)


def _inproj_kernel(h_ref, w_ref, o_ref, *, n_gelu, n_ln, n_plain, group):
    j = pl.program_id(1)

    def run(epilogue):
        for s in range(o_ref.shape[1] // group):
            cs = slice(s * group, (s + 1) * group)
            acc = _dot(h_ref[...], w_ref[:, cs].astype(BF16))
            o_ref[:, cs] = epilogue(acc).astype(o_ref.dtype)

    def group_norm(acc):
        a = jax.nn.gelu(acc)
        cen = a - jnp.mean(a, axis=-1, keepdims=True)
        var = jnp.mean(cen * cen, axis=-1, keepdims=True)
        return cen * lax.rsqrt(var + EPS)

    @pl.when(j < n_gelu)
    def _():
        run(jax.nn.gelu)

    @pl.when((j >= n_gelu) & (j < n_ln))
    def _():
        run(group_norm)

    @pl.when((j >= n_ln) & (j < n_plain))
    def _():
        run(lambda acc: acc)

    @pl.when(j >= n_plain)
    def _():
        run(jax.nn.sigmoid)


def _inproj(h, w_in, a_width, b_width, bm=1024, bn=512):
    n, d = h.shape
    cols = w_in.shape[1]
    group = a_width // A_GROUPS
    kern = functools.partial(
        _inproj_kernel,
        n_gelu=a_width // bn,
        n_ln=2 * a_width // bn,
        n_plain=(2 * a_width + 3 * b_width) // bn,
        group=group,
    )
    return pl.pallas_call(
        kern,
        grid=(n // bm, cols // bn),
        in_specs=[
            pl.BlockSpec((bm, d), lambda i, j: (i, 0)),
            pl.BlockSpec((d, bn), lambda i, j: (0, j)),
        ],
        out_specs=pl.BlockSpec((bm, bn), lambda i, j: (i, j)),
        out_shape=jax.ShapeDtypeStruct((n, cols), BF16),
        compiler_params=_cparams(("arbitrary", "arbitrary")),
        name="inproj",
    )(h, w_in)


def _sgu_kernel(u_ref, v_ref, ws_ref, bs_ref, o_ref, *, group):
    t = A_CHUNK
    row = lax.broadcasted_iota(I32, (t, t), 0)
    col = lax.broadcasted_iota(I32, (t, t), 1)
    for g in range(A_GROUPS):
        ws = jnp.where(row >= col, ws_ref[g], 0.0).astype(BF16)
        bias = bs_ref[:, g:g + 1]
        for c in range(u_ref.shape[0] // t):
            rs = slice(c * t, (c + 1) * t)
            cs = slice(g * group, (g + 1) * group)
            mixed = _dot(ws, v_ref[rs, cs]) + bias
            o_ref[rs, cs] = (u_ref[rs, cs].astype(F32) * mixed).astype(o_ref.dtype)


def _sgu(proj, w_s, b_s, a_width, bm=512):
    n = proj.shape[0]
    group = a_width // A_GROUPS
    return pl.pallas_call(
        functools.partial(_sgu_kernel, group=group),
        grid=(n // bm,),
        in_specs=[
            pl.BlockSpec((bm, a_width), lambda i: (i, 0)),
            pl.BlockSpec((bm, a_width), lambda i: (i, 1)),
            pl.BlockSpec((A_GROUPS, A_CHUNK, A_CHUNK), lambda i: (0, 0, 0)),
            pl.BlockSpec((A_CHUNK, A_GROUPS), lambda i: (0, 0)),
        ],
        out_specs=pl.BlockSpec((bm, a_width), lambda i: (i, 0)),
        out_shape=jax.ShapeDtypeStruct((n, a_width), BF16),
        compiler_params=_cparams(("arbitrary",)),
        name="sgu",
    )(proj, proj, w_s, b_s.T)


def _attn_kernel(q_ref, k_ref, v_ref, o_ref, *, blk, heads, scale):
    seq = q_ref.shape[0]
    row = lax.broadcasted_iota(I32, (blk, blk), 0)
    col = lax.broadcasted_iota(I32, (blk, blk), 1)
    upper = jnp.where(row > col, 1.0, 0.0).astype(BF16)
    strict = col < row

    hslices = [slice(h * HEAD_DIM, (h + 1) * HEAD_DIM) for h in range(heads)]

    def update(qv, ks, cs, accs, diagonal):
        zs = [_dot_nt(qv[h], k_ref[pl.ds(ks, blk), hslices[h]]) * scale for h in range(heads)]
        sps = [jnp.maximum(z, 0.0) + jnp.log(1.0 + jnp.exp(-jnp.abs(z))) for z in zs]
        stays = [jnp.where(strict, -sp, 0.0) if diagonal else -sp for sp in sps]
        his = [s.astype(BF16) for s in stays]
        los = [(s - hi.astype(F32)).astype(BF16) for s, hi in zip(stays, his)]
        laters = [_dot(hi, upper) + _dot(lo, upper) for hi, lo in zip(his, los)]
        ws = [jnp.exp((zs[h] - sps[h]) + laters[h] + cs[h]) for h in range(heads)]
        if diagonal:
            ws = [jnp.where(strict, w, 0.0) for w in ws]
        accs = tuple(accs[h] + _dot(ws[h].astype(BF16), v_ref[pl.ds(ks, blk), hslices[h]])
                     for h in range(heads))
        cs = tuple(cs[h] + laters[h][:, :1] + stays[h][:, :1] for h in range(heads))
        return cs, accs

    def q_block(qi, _):
        qs = pl.multiple_of(qi * blk, blk)
        qv = [q_ref[pl.ds(qs, blk), hs] for hs in hslices]
        cs, accs = update(qv, qs, (jnp.zeros((blk, 1), F32),) * heads,
                          (jnp.zeros((blk, HEAD_DIM), F32),) * heads, True)

        def live(carry):
            n, cs, _ = carry
            cmax = jnp.max(cs[0])
            for c in cs[1:]:
                cmax = jnp.maximum(cmax, jnp.max(c))
            return (n <= qi) & (cmax > DEAD_LOG)

        def older_block(carry):
            n, cs, accs = carry
            ks = pl.multiple_of((qi - n) * blk, blk)
            cs, accs = update(qv, ks, cs, accs, False)
            return n + 1, cs, accs

        _, _, accs = lax.while_loop(live, older_block, (jnp.int32(1), cs, accs))
        for h in range(heads):
            o_ref[pl.ds(qs, blk), hslices[h]] = accs[h].astype(o_ref.dtype)
        return 0

    lax.fori_loop(0, seq // blk, q_block, 0)


def _attention(proj, batch, seq, q_col, k_col, v_col, n_heads, blk=256, heads=4):
    width = heads * HEAD_DIM
    qc, kc, vc = q_col // width, k_col // width, v_col // width
    return pl.pallas_call(
        functools.partial(_attn_kernel, blk=blk, heads=heads, scale=HEAD_DIM ** -0.5),
        grid=(batch, n_heads // heads),
        in_specs=[
            pl.BlockSpec((seq, width), lambda b, h: (b, qc + h)),
            pl.BlockSpec((seq, width), lambda b, h: (b, kc + h)),
            pl.BlockSpec((seq, width), lambda b, h: (b, vc + h)),
        ],
        out_specs=pl.BlockSpec((seq, width), lambda b, h: (b, h)),
        out_shape=jax.ShapeDtypeStruct((batch * seq, n_heads * HEAD_DIM), BF16),
        compiler_params=_cparams(("arbitrary", "arbitrary")),
        name="attn",
    )(proj, proj, proj)


def _merge_kernel(a_ref, b_ref, wa_ref, wb_ref, ga_ref, gb_ref, o_ref):
    ya = _dot(a_ref[...], wa_ref[...].astype(BF16))
    yb = _dot(b_ref[...], wb_ref[...].astype(BF16))
    o_ref[...] = (ga_ref[...].astype(F32) * ya + gb_ref[...].astype(F32) * yb).astype(o_ref.dtype)


def _merge(sgu, att, w_up_a, w_up_b, proj, ga_col, gb_col, bm=1024, bn=512):
    n, ka = sgu.shape
    kb = att.shape[1]
    d = w_up_a.shape[1]
    gac, gbc = ga_col // bn, gb_col // bn
    return pl.pallas_call(
        _merge_kernel,
        grid=(n // bm, d // bn),
        in_specs=[
            pl.BlockSpec((bm, ka), lambda i, j: (i, 0)),
            pl.BlockSpec((bm, kb), lambda i, j: (i, 0)),
            pl.BlockSpec((ka, bn), lambda i, j: (0, j)),
            pl.BlockSpec((kb, bn), lambda i, j: (0, j)),
            pl.BlockSpec((bm, bn), lambda i, j: (i, gac + j)),
            pl.BlockSpec((bm, bn), lambda i, j: (i, gbc + j)),
        ],
        out_specs=pl.BlockSpec((bm, bn), lambda i, j: (i, j)),
        out_shape=jax.ShapeDtypeStruct((n, d), BF16),
        compiler_params=_cparams(("arbitrary", "arbitrary")),
        name="merge",
    )(sgu, att, w_up_a, w_up_b, proj, proj)


def _resid_kernel(a_ref, w_ref, x_ref, mod_ref, o_ref, *, gate_idx):
    y = _dot(a_ref[...], w_ref[...].astype(BF16))
    gate = mod_ref[0][gate_idx:gate_idx + 1]
    o_ref[...] = x_ref[...] + gate * y


def _resid_matmul(a, w, x2, mod, gate_idx, seq, name, bm=1024, bn=512):
    n, k = a.shape
    d = w.shape[1]
    return pl.pallas_call(
        functools.partial(_resid_kernel, gate_idx=gate_idx),
        grid=(n // bm, d // bn),
        in_specs=[
            pl.BlockSpec((bm, k), lambda i, j: (i, 0)),
            pl.BlockSpec((k, bn), lambda i, j: (0, j)),
            pl.BlockSpec((bm, bn), lambda i, j: (i, j)),
            pl.BlockSpec((1, N_MOD, bn), lambda i, j: (i * bm // seq, 0, j)),
        ],
        out_specs=pl.BlockSpec((bm, bn), lambda i, j: (i, j)),
        out_shape=jax.ShapeDtypeStruct((n, d), F32),
        compiler_params=_cparams(("arbitrary", "arbitrary")),
        name=name,
    )(a, w, x2, mod)


def _first_argmax(vals, iota, size):
    m = jnp.max(vals, axis=0, keepdims=True)
    idx = jnp.min(jnp.where(vals == m, iota, size), axis=0, keepdims=True)
    return m, idx


def _router_kernel(x_ref, g_ref, mod_ref, wr_ref, rb_ref, h_ref, hp_ref, ids_ref, wts_ref, rank_ref,
                   cnt_ref, carry_ref):
    i = pl.program_id(0)
    bm = x_ref.shape[0]
    e = N_EXPERTS
    per = e // N_GROUPS

    @pl.when(i == 0)
    def _():
        carry_ref[...] = jnp.zeros_like(carry_ref)

    h = _norm_mod(x_ref[...], g_ref, mod_ref, 3, 4)
    h_ref[...] = h.astype(BF16)
    hp_ref[...] = _pack_halves(h)

    h_hi = h.astype(BF16)
    h_lo = (h - h_hi.astype(F32)).astype(BF16)
    wr = wr_ref[...]
    w_hi = wr.astype(BF16)
    w_lo = (wr - w_hi.astype(F32)).astype(BF16)
    logits = _dot_nt(w_hi, h_hi) + (_dot_nt(w_hi, h_lo) + _dot_nt(w_lo, h_hi))
    scores = jax.nn.sigmoid(logits)
    choice = scores + rb_ref[...]

    iota_p = lax.broadcasted_iota(I32, (per, bm), 0)
    gs_rows = []
    for g in range(N_GROUPS):
        cg = choice[g * per:(g + 1) * per, :]
        m1, i1 = _first_argmax(cg, iota_p, per)
        m2 = jnp.max(jnp.where(iota_p == i1, -jnp.inf, cg), axis=0, keepdims=True)
        gs_rows.append(m1 + m2)
    gs = jnp.concatenate(gs_rows, axis=0)

    iota_g = lax.broadcasted_iota(I32, (N_GROUPS, bm), 0)
    gsel = jnp.zeros((N_GROUPS, bm), F32)
    for _ in range(TOPK_GROUPS):
        _, ig = _first_argmax(gs, iota_g, N_GROUPS)
        hit = iota_g == ig
        gsel = jnp.where(hit, 1.0, gsel)
        gs = jnp.where(hit, -jnp.inf, gs)
    emask = jnp.concatenate(
        [jnp.broadcast_to(gsel[g:g + 1, :], (per, bm)) for g in range(N_GROUPS)], axis=0)

    iota_e = lax.broadcasted_iota(I32, (e, bm), 0)
    masked = jnp.where(emask > 0.5, choice, -jnp.inf)
    self_f = jnp.zeros((e, bm), F32)
    idx_rows, w_rows = [], []
    for _ in range(TOP_K):
        _, ie = _first_argmax(masked, iota_e, e)
        hit = iota_e == ie
        self_f = jnp.where(hit, 1.0, self_f)
        idx_rows.append(ie)
        w_rows.append(jnp.sum(jnp.where(hit, scores, 0.0), axis=0, keepdims=True))
        masked = jnp.where(hit, -jnp.inf, masked)
    w = jnp.concatenate(w_rows, axis=0)
    w = w / jnp.sum(w, axis=0, keepdims=True) * ROUTED_SCALE

    t_row = lax.broadcasted_iota(I32, (bm, bm), 0)
    t_col = lax.broadcasted_iota(I32, (bm, bm), 1)
    before = jnp.where(t_row < t_col, 1.0, 0.0).astype(BF16)
    rank_mat = _dot(self_f.astype(BF16), before) + carry_ref[:, :1]
    rank_rows = [jnp.sum(jnp.where(iota_e == ie, rank_mat, 0.0), axis=0, keepdims=True)
                 for ie in idx_rows]
    carry_ref[...] = carry_ref[...] + jnp.sum(self_f, axis=1, keepdims=True)

    pad_i = jnp.zeros((K_PAD - TOP_K, bm), I32)
    ids_ref[...] = jnp.concatenate(idx_rows + [pad_i], axis=0)
    wts_ref[...] = jnp.concatenate([w, jnp.zeros((K_PAD - TOP_K, bm), F32)], axis=0)
    rank_ref[...] = jnp.concatenate([r.astype(I32) for r in rank_rows] + [pad_i], axis=0)
    cnt_ref[...] = carry_ref[...].astype(I32)


def _router(x1, g, mod, w_router, router_bias, seq, bm=512):
    n, d = x1.shape
    e = N_EXPERTS
    tok = lambda i: (0, i)
    return pl.pallas_call(
        _router_kernel,
        grid=(n // bm,),
        in_specs=[
            pl.BlockSpec((bm, d), lambda i: (i, 0)),
            pl.BlockSpec((1, d), lambda i: (0, 0)),
            pl.BlockSpec((1, N_MOD, d), lambda i: (i * bm // seq, 0, 0)),
            pl.BlockSpec((e, d), lambda i: (0, 0)),
            pl.BlockSpec((e, 1), lambda i: (0, 0)),
        ],
        out_specs=[
            pl.BlockSpec((bm, d), lambda i: (i, 0)),
            pl.BlockSpec((bm, d // 2), lambda i: (i, 0)),
            pl.BlockSpec((K_PAD, bm), tok),
            pl.BlockSpec((K_PAD, bm), tok),
            pl.BlockSpec((K_PAD, bm), tok),
            pl.BlockSpec((e, 128), lambda i: (0, 0)),
        ],
        out_shape=[
            jax.ShapeDtypeStruct((n, d), BF16),
            jax.ShapeDtypeStruct((n, d // 2), U32),
            jax.ShapeDtypeStruct((K_PAD, n), I32),
            jax.ShapeDtypeStruct((K_PAD, n), F32),
            jax.ShapeDtypeStruct((K_PAD, n), I32),
            jax.ShapeDtypeStruct((e, 128), I32),
        ],
        scratch_shapes=[pltpu.VMEM((e, 128), F32)],
        compiler_params=_cparams(("arbitrary",)),
        name="router",
    )(x1, g.reshape(1, d), mod, w_router.T, router_bias.reshape(e, 1))


def _pos_kernel(off_ref, ids_ref, rank_ref, o_ref):
    ids = ids_ref[...]
    pos = rank_ref[...]
    for ex in range(N_EXPERTS):
        pos = pos + jnp.where(ids == ex, off_ref[ex], 0)
    o_ref[...] = pos


def _positions(offsets, ids, ranks):
    return pl.pallas_call(
        _pos_kernel,
        in_specs=[
            pl.BlockSpec(memory_space=pltpu.SMEM),
            pl.BlockSpec(memory_space=pltpu.VMEM),
            pl.BlockSpec(memory_space=pltpu.VMEM),
        ],
        out_specs=pl.BlockSpec(memory_space=pltpu.VMEM),
        out_shape=jax.ShapeDtypeStruct(ids.shape, I32),
        name="positions",
    )(offsets, ids, ranks)


def _dispatch_kernel(pad_ref, pos_ref, h_ref, xs_ref, zeros, sem, pad_sem, *, tm):
    bt = h_ref.shape[0]

    @pl.when(pl.program_id(0) == 0)
    def _():
        zeros[...] = jnp.zeros_like(zeros)
        group_bits = [1 << b for b in reversed(range((tm // SUBLANES - 1).bit_length()))]

        def pad_copies(ex):
            start, gap = pad_ref[0, ex], pad_ref[1, ex]
            end = start + gap
            groups, singles = gap >> 3, gap & (SUBLANES - 1)
            out = []
            for j in range(SUBLANES - 1):
                out.append((j < singles, pltpu.make_async_copy(
                    zeros.at[pl.ds(0, 1)], xs_ref.at[pl.ds(start + j, 1)], pad_sem)))
            for bit in group_bits:
                first = pl.multiple_of(end - SUBLANES * (groups & (2 * bit - 1)), SUBLANES)
                out.append(((groups & bit) != 0, pltpu.make_async_copy(
                    zeros.at[pl.ds(0, SUBLANES * bit)], xs_ref.at[pl.ds(first, SUBLANES * bit)], pad_sem)))
            return out

        def issue_pad(ex, _):
            for cond, cp in pad_copies(ex):
                @pl.when(cond)
                def _():
                    cp.start()
            return 0

        def drain_pad(ex, _):
            for cond, cp in pad_copies(ex):
                @pl.when(cond)
                def _():
                    cp.wait()
            return 0

        lax.fori_loop(0, N_EXPERTS, issue_pad, 0)
        lax.fori_loop(0, N_EXPERTS, drain_pad, 0)

        zrows = zeros.shape[0]
        first = (pad_ref[0, N_EXPERTS - 1] + pad_ref[1, N_EXPERTS - 1]) // zrows

        def tail_copy(g):
            return pltpu.make_async_copy(zeros, xs_ref.at[pl.ds(pl.multiple_of(g * zrows, zrows), zrows)],
                                         pad_sem)

        def issue_tail(g, _):
            tail_copy(g).start()
            return 0

        def drain_tail(g, _):
            tail_copy(g).wait()
            return 0

        lax.fori_loop(first, xs_ref.shape[0] // zrows, issue_tail, 0)
        lax.fori_loop(first, xs_ref.shape[0] // zrows, drain_tail, 0)

    def issue(r, _):
        for k in range(TOP_K):
            p = pos_ref[0, k, r]
            pltpu.make_async_copy(h_ref.at[pl.ds(r, 1)], xs_ref.at[pl.ds(p, 1)], sem).start(
                priority=k % DMA_QUEUES)
        return 0

    lax.fori_loop(0, bt, issue, 0, unroll=ROW_DMA_UNROLL)
    for _ in range(TOP_K):
        pltpu.make_async_copy(h_ref, xs_ref.at[pl.ds(0, bt)], sem).wait()


def _dispatch(hp, pos3, pads, n_rows, bt, tm):
    n, c = hp.shape
    return pl.pallas_call(
        functools.partial(_dispatch_kernel, tm=tm),
        grid=(n // bt,),
        in_specs=[
            pl.BlockSpec(memory_space=pltpu.SMEM),
            pl.BlockSpec((1, K_PAD, bt), lambda i: (i, 0, 0), memory_space=pltpu.SMEM),
            pl.BlockSpec((bt, c), lambda i: (i, 0)),
        ],
        out_specs=pl.BlockSpec(memory_space=pl.ANY),
        out_shape=jax.ShapeDtypeStruct((n_rows, c), U32),
        scratch_shapes=[pltpu.VMEM((tm // 2, c), U32), pltpu.SemaphoreType.DMA(()),
                        pltpu.SemaphoreType.DMA(())],
        compiler_params=_cparams(("arbitrary",)),
        name="dispatch",
    )(pads, pos3, hp)


def _expert_tiles(ts_ref, src_ref, dst_ref, in_buf, out_buf, zero_buf, in_sem, out_sem, zero_sem, compute):
    e = pl.program_id(0)
    n_exp = pl.num_programs(0)
    n_in, tm = in_buf.shape[0], in_buf.shape[1]
    n_out = out_buf.shape[0]
    n_tiles = dst_ref.shape[0] // tm
    n_used = ts_ref[n_exp]

    def rows(g):
        return pl.ds(pl.multiple_of(g * tm, tm), tm)

    def load(g):
        return pltpu.make_async_copy(src_ref.at[rows(g)], in_buf.at[g % n_in], in_sem.at[g % n_in])

    def store(g):
        return pltpu.make_async_copy(out_buf.at[g % n_out], dst_ref.at[rows(g)], out_sem.at[g % n_out])

    @pl.when(e == 0)
    def _():
        for g in range(n_in - 1):
            @pl.when(g < n_used)
            def _():
                load(g).start(priority=TILE_DMA_PRIORITY)

    def tile(g, _):
        @pl.when(g + n_in - 1 < n_used)
        def _():
            load(g + n_in - 1).start(priority=TILE_DMA_PRIORITY)

        load(g).wait()

        @pl.when(g >= n_out)
        def _():
            store(g - n_out).wait()

        out_buf[g % n_out] = compute(in_buf[g % n_in])
        store(g).start(priority=TILE_DMA_PRIORITY)
        return 0

    lax.fori_loop(ts_ref[e], ts_ref[e + 1], tile, 0)

    @pl.when(e == n_exp - 1)
    def _():
        for back in range(n_out, 0, -1):
            @pl.when(n_used >= back)
            def _():
                store(n_used - back).wait()

        zero_buf[...] = jnp.zeros_like(zero_buf)

        def fill(g):
            return pltpu.make_async_copy(zero_buf, dst_ref.at[rows(g)], zero_sem)

        def start_fill(g, _):
            fill(g).start()
            return 0

        def wait_fill(g, _):
            fill(g).wait()
            return 0

        lax.fori_loop(n_used, n_tiles, start_fill, 0)
        lax.fori_loop(n_used, n_tiles, wait_fill, 0)


def _expert_up_kernel(ts_ref, wg_ref, wu_ref, xs_ref, act_ref, w_scr, in_buf, out_buf, zero_buf,
                      in_sem, out_sem, zero_sem):
    e = pl.program_id(0)
    f = wg_ref.shape[2]

    @pl.when(ts_ref[e + 1] > ts_ref[e])
    def _():
        w_scr[:, :f] = wg_ref[0].astype(BF16)
        w_scr[:, f:] = wu_ref[0].astype(BF16)

    def compute(xp):
        lo, hi = _unpack_halves(xp)
        x = jnp.concatenate([lo.astype(BF16), hi.astype(BF16)], axis=1)
        gu = _dot(x, w_scr[...])
        g, u = gu[:, :f], gu[:, f:]
        return (g * jax.nn.sigmoid(g) * u).astype(BF16)

    _expert_tiles(ts_ref, xs_ref, act_ref, in_buf, out_buf, zero_buf, in_sem, out_sem, zero_sem, compute)


def _expert_up(tile_start, xs, w_gate_e, w_up_e, tm, in_slots=4, out_slots=8):
    p, c = xs.shape
    e, d, f = w_gate_e.shape
    grid_spec = pltpu.PrefetchScalarGridSpec(
        num_scalar_prefetch=1,
        grid=(e,),
        in_specs=[
            pl.BlockSpec((1, d, f), lambda i, ts: (i, 0, 0)),
            pl.BlockSpec((1, d, f), lambda i, ts: (i, 0, 0)),
            pl.BlockSpec(memory_space=pl.ANY),
        ],
        out_specs=pl.BlockSpec(memory_space=pl.ANY),
        scratch_shapes=[
            pltpu.VMEM((d, 2 * f), BF16),
            pltpu.VMEM((in_slots, tm, c), U32),
            pltpu.VMEM((out_slots, tm, f), BF16),
            pltpu.VMEM((tm, f), BF16),
            pltpu.SemaphoreType.DMA((in_slots,)),
            pltpu.SemaphoreType.DMA((out_slots,)),
            pltpu.SemaphoreType.DMA(()),
        ],
    )
    return pl.pallas_call(
        _expert_up_kernel,
        grid_spec=grid_spec,
        out_shape=jax.ShapeDtypeStruct((p, f), BF16),
        compiler_params=_cparams(("arbitrary",)),
        name="expert_up",
    )(tile_start, w_gate_e, w_up_e, xs)


def _expert_down_kernel(ts_ref, wd_ref, act_ref, ys_ref, w_scr, in_buf, out_buf, zero_buf,
                        in_sem, out_sem, zero_sem):
    e = pl.program_id(0)

    @pl.when(ts_ref[e + 1] > ts_ref[e])
    def _():
        w_scr[...] = wd_ref[0].astype(BF16)

    def compute(a):
        return _pack_halves(_dot(a, w_scr[...]))

    _expert_tiles(ts_ref, act_ref, ys_ref, in_buf, out_buf, zero_buf, in_sem, out_sem, zero_sem, compute)


def _expert_down(tile_start, act, w_down_e, tm, in_slots=8, out_slots=8):
    p, f = act.shape
    e, _, d = w_down_e.shape
    grid_spec = pltpu.PrefetchScalarGridSpec(
        num_scalar_prefetch=1,
        grid=(e,),
        in_specs=[
            pl.BlockSpec((1, f, d), lambda i, ts: (i, 0, 0)),
            pl.BlockSpec(memory_space=pl.ANY),
        ],
        out_specs=pl.BlockSpec(memory_space=pl.ANY),
        scratch_shapes=[
            pltpu.VMEM((f, d), BF16),
            pltpu.VMEM((in_slots, tm, f), BF16),
            pltpu.VMEM((out_slots, tm, d // 2), U32),
            pltpu.VMEM((tm, d // 2), U32),
            pltpu.SemaphoreType.DMA((in_slots,)),
            pltpu.SemaphoreType.DMA((out_slots,)),
            pltpu.SemaphoreType.DMA(()),
        ],
    )
    return pl.pallas_call(
        _expert_down_kernel,
        grid_spec=grid_spec,
        out_shape=jax.ShapeDtypeStruct((p, d // 2), U32),
        compiler_params=_cparams(("arbitrary",)),
        name="expert_down",
    )(tile_start, w_down_e, act)


def _shared_up_kernel(h_ref, wg_ref, wu_ref, o_ref):
    h = h_ref[...]
    g = _dot(h, wg_ref[...].astype(BF16))
    u = _dot(h, wu_ref[...].astype(BF16))
    o_ref[...] = (g * jax.nn.sigmoid(g) * u).astype(o_ref.dtype)


def _shared_up(h, w_gate_s, w_up_s, bm=1024, bn=256):
    n, d = h.shape
    f = w_gate_s.shape[1]
    return pl.pallas_call(
        _shared_up_kernel,
        grid=(n // bm, f // bn),
        in_specs=[
            pl.BlockSpec((bm, d), lambda i, j: (i, 0)),
            pl.BlockSpec((d, bn), lambda i, j: (0, j)),
            pl.BlockSpec((d, bn), lambda i, j: (0, j)),
        ],
        out_specs=pl.BlockSpec((bm, bn), lambda i, j: (i, j)),
        out_shape=jax.ShapeDtypeStruct((n, f), BF16),
        compiler_params=_cparams(("arbitrary", "arbitrary")),
        name="shared_up",
    )(h, w_gate_s, w_up_s)


def _combine_kernel(pos_ref, pos_next_ref, x_ref, s_ref, wt_ref, mod_ref, g_ref, ys_ref, wd_ref, o_ref,
                    ybuf, wd_scr, stage, sem, wd_sem):
    i = pl.program_id(0)
    bt = x_ref.shape[0]
    slot = i % 2
    assert bt % SUBLANES == 0

    @pl.when(i == 0)
    def _():
        rows = stage.shape[0]
        for part in range(wd_scr.shape[0] // rows):
            cp = pltpu.make_async_copy(wd_ref.at[pl.ds(part * rows, rows)], stage, wd_sem)
            cp.start()
            cp.wait()
            wd_scr[part * rows:(part + 1) * rows, :] = stage[...].astype(BF16)

    def row_gather(p_ref, r, dst_slot):
        for k in range(TOP_K):
            p = p_ref[0, k, r]
            pltpu.make_async_copy(ys_ref.at[pl.ds(p, 1)], ybuf.at[dst_slot, k, pl.ds(r, 1)],
                                  sem.at[dst_slot]).start(priority=k % DMA_QUEUES)

    def wait_gather(dst_slot):
        for k in range(TOP_K):
            pltpu.make_async_copy(ys_ref.at[pl.ds(0, bt)], ybuf.at[dst_slot, k], sem.at[dst_slot]).wait()

    @pl.when(i == 0)
    def _():
        def first(r, _):
            row_gather(pos_ref, r, 0)
            return 0

        lax.fori_loop(0, bt, first, 0, unroll=ROW_DMA_UNROLL)

    wait_gather(slot)
    shared = _dot(s_ref[...], wd_scr[...])
    gate = mod_ref[0][5:6]
    c = ybuf.shape[3]
    for j in range(bt // SUBLANES):
        for r in range(j * SUBLANES, (j + 1) * SUBLANES):
            row_gather(pos_next_ref, r, 1 - slot)
        rs = slice(j * SUBLANES, (j + 1) * SUBLANES)
        acc_lo = jnp.zeros((SUBLANES, c), F32)
        acc_hi = jnp.zeros((SUBLANES, c), F32)
        for k in range(TOP_K):
            lo, hi = _unpack_halves(ybuf[slot, k, rs])
            wk = wt_ref[rs, k:k + 1]
            acc_lo = acc_lo + wk * lo
            acc_hi = acc_hi + wk * hi
        moe = jnp.concatenate([acc_lo, acc_hi], axis=1) + shared[rs]
        x = x_ref[rs, :] + gate * moe
        o_ref[rs, :] = x * lax.rsqrt(jnp.mean(x * x, axis=-1, keepdims=True) + EPS) * g_ref[...]

    @pl.when(i == pl.num_programs(0) - 1)
    def _():
        wait_gather(1 - slot)


def _combine(pos3, x1, sact, wts_t, mod, final_g, ys, w_down_s, seq, bt, stage_rows=128):
    n, d = x1.shape
    f = sact.shape[1]
    nb = n // bt
    pos_spec = lambda imap: pl.BlockSpec((1, K_PAD, bt), imap, memory_space=pltpu.SMEM)
    return pl.pallas_call(
        _combine_kernel,
        grid=(nb,),
        in_specs=[
            pos_spec(lambda i: (i, 0, 0)),
            pos_spec(lambda i: (jnp.minimum(i + 1, nb - 1), 0, 0)),
            pl.BlockSpec((bt, d), lambda i: (i, 0)),
            pl.BlockSpec((bt, f), lambda i: (i, 0)),
            pl.BlockSpec((bt, K_PAD), lambda i: (i, 0)),
            pl.BlockSpec((1, N_MOD, d), lambda i: (i * bt // seq, 0, 0)),
            pl.BlockSpec((1, d), lambda i: (0, 0)),
            pl.BlockSpec(memory_space=pl.ANY),
            pl.BlockSpec(memory_space=pl.ANY),
        ],
        out_specs=pl.BlockSpec((bt, d), lambda i: (i, 0)),
        out_shape=jax.ShapeDtypeStruct((n, d), F32),
        scratch_shapes=[
            pltpu.VMEM((2, TOP_K, bt, d // 2), U32),
            pltpu.VMEM((f, d), BF16),
            pltpu.VMEM((stage_rows, d), F32),
            pltpu.SemaphoreType.DMA((2,)),
            pltpu.SemaphoreType.DMA(()),
        ],
        compiler_params=_cparams(("arbitrary",)),
        name="combine",
    )(pos3, pos3, x1, sact, wts_t, mod, final_g.reshape(1, d), ys, w_down_s)


def _moe_tiles(counts, n_tokens, tm):
    e = counts.shape[0]
    n_tiles = n_tokens * TOP_K // tm + e
    tiles_per = (counts + tm - 1) // tm
    tile_end = jnp.cumsum(tiles_per)
    tile_start = jnp.concatenate([jnp.zeros((1,), I32), tile_end.astype(I32)])
    offsets = tile_start[:-1] * tm
    pads = jnp.stack([offsets + counts, tiles_per * tm - counts]).astype(I32)
    return offsets, tile_start, pads, n_tiles


def kernel(x, c, w_ada, b_ada, norm1_g, w_in, w_s, b_s, w_up_a, w_up_b, w_o, norm2_g, w_router,
           router_bias, w_gate_e, w_up_e, w_down_e, w_gate_s, w_up_s, w_down_s, final_g):
    batch, seq, d = x.shape
    n = batch * seq
    depth = w_ada.shape[0]
    a_width = w_up_a.shape[1]
    b_width = w_up_b.shape[1]
    n_heads = b_width // HEAD_DIM
    tm = 256
    bt_dispatch = 512
    bt_combine = 128

    assert depth == 1, "the combine kernel applies the final norm, so only one layer is supported"
    l = 0
    x2 = x.reshape(n, d)
    mod = _ada(c, w_ada[l], b_ada[l])
    h1 = _norm1(x2, norm1_g[l], mod, seq)
    proj = _inproj(h1, w_in[l], a_width, b_width)
    sgu = _sgu(proj, w_s[l], b_s[l], a_width)
    att = _attention(proj, batch, seq, 2 * a_width, 2 * a_width + b_width,
                     2 * a_width + 2 * b_width, n_heads)
    merged = _merge(sgu, att, w_up_a[l], w_up_b[l], proj,
                    2 * a_width + 3 * b_width, 2 * a_width + 3 * b_width + d)
    x1 = _resid_matmul(merged, w_o[l], x2, mod, 2, seq, "out_proj")

    h2, h2p, ids, wts, ranks, counts = _router(x1, norm2_g[l], mod, w_router[l], router_bias[l], seq)
    offsets, tile_start, pads, n_tiles = _moe_tiles(counts[:, 0], n, tm)
    pos = _positions(offsets, ids, ranks)
    blocked = lambda b: pos.reshape(K_PAD, n // b, b).transpose(1, 0, 2)
    xs = _dispatch(h2p, blocked(bt_dispatch), pads, n_tiles * tm, bt_dispatch, tm)
    act = _expert_up(tile_start, xs, w_gate_e[l], w_up_e[l], tm)
    ys = _expert_down(tile_start, act, w_down_e[l], tm)
    sact = _shared_up(h2, w_gate_s[l], w_up_s[l])
    out = _combine(blocked(bt_combine), x1, sact, wts.T, mod, final_g, ys, w_down_s[l], seq, bt_combine)
    return out.reshape(batch, seq, d)
```

```python
import functools

import jax
import jax.numpy as jnp
from jax import lax
from jax.experimental import pallas as pl
from jax.experimental.pallas import tpu as pltpu

F32 = jnp.float32
BF16 = jnp.bfloat16
I32 = jnp.int32
U32 = jnp.uint32

EPS = 1e-6
A_GROUPS = 8
A_CHUNK = 128
HEAD_DIM = 128
N_EXPERTS = 64
TOP_K = 6
N_GROUPS = 8
TOPK_GROUPS = 4
ROUTED_SCALE = 2.5
N_MOD = 6
SUBLANES = 8
K_PAD = SUBLANES
SHARED_K_PIECES = 4
ROW_DMA_UNROLL = 8
DMA_QUEUES = 2
TILE_DMA_PRIORITY = 1
DEAD_LOG = -105.0

VMEM_LIMIT = 56 * 1024 * 1024


def _cparams(sem):
    return pltpu.CompilerParams(dimension_semantics=sem, vmem_limit_bytes=VMEM_LIMIT)


def _dot(a, b):
    return jnp.dot(a, b, preferred_element_type=F32)


def _dot_nt(a, b):
    return lax.dot_general(a, b, (((1,), (1,)), ((), ())), preferred_element_type=F32)


def _pack_halves(h):
    c = h.shape[1] // 2
    lo = pltpu.bitcast(h[:, :c].astype(BF16).astype(F32), U32)
    hi = pltpu.bitcast(h[:, c:].astype(BF16).astype(F32), U32)
    return (hi & jnp.uint32(0xFFFF0000)) | (lo >> 16)


def _unpack_halves(p):
    lo = pltpu.bitcast(p << 16, F32)
    hi = pltpu.bitcast(p & jnp.uint32(0xFFFF0000), F32)
    return lo, hi


def _ada_kernel(c_ref, w_ref, b_ref, o_ref):
    c = c_ref[...]
    cact = (c * jax.nn.sigmoid(c)).astype(BF16)
    o_ref[...] = _dot(cact, w_ref[...].astype(BF16)) + b_ref[...]


def _ada(c, w_ada, b_ada, bn=512):
    b, d = c.shape
    n = w_ada.shape[1]
    cp = jnp.zeros((8, d), F32).at[:b].set(c)
    out = pl.pallas_call(
        _ada_kernel,
        grid=(n // bn,),
        in_specs=[
            pl.BlockSpec((8, d), lambda j: (0, 0)),
            pl.BlockSpec((d, bn), lambda j: (0, j)),
            pl.BlockSpec((1, bn), lambda j: (0, j)),
        ],
        out_specs=pl.BlockSpec((8, bn), lambda j: (0, j)),
        out_shape=jax.ShapeDtypeStruct((8, n), F32),
        compiler_params=_cparams(("arbitrary",)),
        name="ada",
    )(cp, w_ada, b_ada.reshape(1, n))
    return out[:b].reshape(b, N_MOD, d)


def _norm_mod(x, g_ref, mod_ref, shift_idx, scale_idx):
    y = x * lax.rsqrt(jnp.mean(x * x, axis=-1, keepdims=True) + EPS) * g_ref[...]
    mod = mod_ref[0]
    return y * (1.0 + mod[scale_idx:scale_idx + 1]) + mod[shift_idx:shift_idx + 1]


def _norm1_kernel(x_ref, g_ref, mod_ref, o_ref):
    o_ref[...] = _norm_mod(x_ref[...], g_ref, mod_ref, 0, 1).astype(BF16)


def _norm1(x2, g, mod, seq, bm=512):
    n, d = x2.shape
    return pl.pallas_call(
        _norm1_kernel,
        grid=(n // bm,),
        in_specs=[
            pl.BlockSpec((bm, d), lambda i: (i, 0)),
            pl.BlockSpec((1, d), lambda i: (0, 0)),
            pl.BlockSpec((1, N_MOD, d), lambda i: (i * bm // seq, 0, 0)),
        ],
        out_specs=pl.BlockSpec((bm, d), lambda i: (i, 0)),
        out_shape=jax.ShapeDtypeStruct((n, d), BF16),
        compiler_params=_cparams(("arbitrary",)),
        name="norm1",
    )(x2, g.reshape(1, d), mod)


def _inproj_kernel(h_ref, w_ref, o_ref, *, n_gelu, n_ln, n_plain, group):
    j = pl.program_id(1)

    def run(epilogue):
        for s in range(o_ref.shape[1] // group):
            cs = slice(s * group, (s + 1) * group)
            acc = _dot(h_ref[...], w_ref[:, cs].astype(BF16))
            o_ref[:, cs] = epilogue(acc).astype(o_ref.dtype)

    def group_norm(acc):
        a = jax.nn.gelu(acc)
        cen = a - jnp.mean(a, axis=-1, keepdims=True)
        var = jnp.mean(cen * cen, axis=-1, keepdims=True)
        return cen * lax.rsqrt(var + EPS)

    @pl.when(j < n_gelu)
    def _():
        run(jax.nn.gelu)

    @pl.when((j >= n_gelu) & (j < n_ln))
    def _():
        run(group_norm)

    @pl.when((j >= n_ln) & (j < n_plain))
    def _():
        run(lambda acc: acc)

    @pl.when(j >= n_plain)
    def _():
        run(jax.nn.sigmoid)


def _inproj(h, w_in, a_width, b_width, bm=1024, bn=512):
    n, d = h.shape
    cols = w_in.shape[1]
    group = a_width // A_GROUPS
    kern = functools.partial(
        _inproj_kernel,
        n_gelu=a_width // bn,
        n_ln=2 * a_width // bn,
        n_plain=(2 * a_width + 3 * b_width) // bn,
        group=group,
    )
    return pl.pallas_call(
        kern,
        grid=(n // bm, cols // bn),
        in_specs=[
            pl.BlockSpec((bm, d), lambda i, j: (i, 0)),
            pl.BlockSpec((d, bn), lambda i, j: (0, j)),
        ],
        out_specs=pl.BlockSpec((bm, bn), lambda i, j: (i, j)),
        out_shape=jax.ShapeDtypeStruct((n, cols), BF16),
        compiler_params=_cparams(("arbitrary", "arbitrary")),
        name="inproj",
    )(h, w_in)


def _sgu_kernel(u_ref, v_ref, ws_ref, bs_ref, o_ref, *, group):
    t = A_CHUNK
    row = lax.broadcasted_iota(I32, (t, t), 0)
    col = lax.broadcasted_iota(I32, (t, t), 1)
    for g in range(A_GROUPS):
        ws = jnp.where(row >= col, ws_ref[g], 0.0).astype(BF16)
        bias = bs_ref[:, g:g + 1]
        for c in range(u_ref.shape[0] // t):
            rs = slice(c * t, (c + 1) * t)
            cs = slice(g * group, (g + 1) * group)
            mixed = _dot(ws, v_ref[rs, cs]) + bias
            o_ref[rs, cs] = (u_ref[rs, cs].astype(F32) * mixed).astype(o_ref.dtype)


def _sgu(proj, w_s, b_s, a_width, bm=512):
    n = proj.shape[0]
    group = a_width // A_GROUPS
    return pl.pallas_call(
        functools.partial(_sgu_kernel, group=group),
        grid=(n // bm,),
        in_specs=[
            pl.BlockSpec((bm, a_width), lambda i: (i, 0)),
            pl.BlockSpec((bm, a_width), lambda i: (i, 1)),
            pl.BlockSpec((A_GROUPS, A_CHUNK, A_CHUNK), lambda i: (0, 0, 0)),
            pl.BlockSpec((A_CHUNK, A_GROUPS), lambda i: (0, 0)),
        ],
        out_specs=pl.BlockSpec((bm, a_width), lambda i: (i, 0)),
        out_shape=jax.ShapeDtypeStruct((n, a_width), BF16),
        compiler_params=_cparams(("arbitrary",)),
        name="sgu",
    )(proj, proj, w_s, b_s.T)


def _attn_kernel(q_ref, k_ref, v_ref, o_ref, *, blk, heads, scale):
    seq = q_ref.shape[0]
    row = lax.broadcasted_iota(I32, (blk, blk), 0)
    col = lax.broadcasted_iota(I32, (blk, blk), 1)
    upper = jnp.where(row > col, 1.0, 0.0).astype(BF16)
    strict = col < row

    hslices = [slice(h * HEAD_DIM, (h + 1) * HEAD_DIM) for h in range(heads)]

    def update(qv, ks, cs, accs, diagonal):
        zs = [_dot_nt(qv[h], k_ref[pl.ds(ks, blk), hslices[h]]) * scale for h in range(heads)]
        sps = [jnp.maximum(z, 0.0) + jnp.log(1.0 + jnp.exp(-jnp.abs(z))) for z in zs]
        stays = [jnp.where(strict, -sp, 0.0) if diagonal else -sp for sp in sps]
        his = [s.astype(BF16) for s in stays]
        los = [(s - hi.astype(F32)).astype(BF16) for s, hi in zip(stays, his)]
        laters = [_dot(hi, upper) + _dot(lo, upper) for hi, lo in zip(his, los)]
        ws = [jnp.exp((zs[h] - sps[h]) + laters[h] + cs[h]) for h in range(heads)]
        if diagonal:
            ws = [jnp.where(strict, w, 0.0) for w in ws]
        accs = tuple(accs[h] + _dot(ws[h].astype(BF16), v_ref[pl.ds(ks, blk), hslices[h]])
                     for h in range(heads))
        cs = tuple(cs[h] + laters[h][:, :1] + stays[h][:, :1] for h in range(heads))
        return cs, accs

    def q_block(qi, _):
        qs = pl.multiple_of(qi * blk, blk)
        qv = [q_ref[pl.ds(qs, blk), hs] for hs in hslices]
        cs, accs = update(qv, qs, (jnp.zeros((blk, 1), F32),) * heads,
                          (jnp.zeros((blk, HEAD_DIM), F32),) * heads, True)

        def live(carry):
            n, cs, _ = carry
            cmax = jnp.max(cs[0])
            for c in cs[1:]:
                cmax = jnp.maximum(cmax, jnp.max(c))
            return (n <= qi) & (cmax > DEAD_LOG)

        def older_block(carry):
            n, cs, accs = carry
            ks = pl.multiple_of((qi - n) * blk, blk)
            cs, accs = update(qv, ks, cs, accs, False)
            return n + 1, cs, accs

        _, _, accs = lax.while_loop(live, older_block, (jnp.int32(1), cs, accs))
        for h in range(heads):
            o_ref[pl.ds(qs, blk), hslices[h]] = accs[h].astype(o_ref.dtype)
        return 0

    lax.fori_loop(0, seq // blk, q_block, 0)


def _attention(proj, batch, seq, q_col, k_col, v_col, n_heads, blk=256, heads=4):
    width = heads * HEAD_DIM
    qc, kc, vc = q_col // width, k_col // width, v_col // width
    return pl.pallas_call(
        functools.partial(_attn_kernel, blk=blk, heads=heads, scale=HEAD_DIM ** -0.5),
        grid=(batch, n_heads // heads),
        in_specs=[
            pl.BlockSpec((seq, width), lambda b, h: (b, qc + h)),
            pl.BlockSpec((seq, width), lambda b, h: (b, kc + h)),
            pl.BlockSpec((seq, width), lambda b, h: (b, vc + h)),
        ],
        out_specs=pl.BlockSpec((seq, width), lambda b, h: (b, h)),
        out_shape=jax.ShapeDtypeStruct((batch * seq, n_heads * HEAD_DIM), BF16),
        compiler_params=_cparams(("arbitrary", "arbitrary")),
        name="attn",
    )(proj, proj, proj)


def _merge_kernel(a_ref, b_ref, wa_ref, wb_ref, ga_ref, gb_ref, o_ref):
    ya = _dot(a_ref[...], wa_ref[...].astype(BF16))
    yb = _dot(b_ref[...], wb_ref[...].astype(BF16))
    o_ref[...] = (ga_ref[...].astype(F32) * ya + gb_ref[...].astype(F32) * yb).astype(o_ref.dtype)


def _merge(sgu, att, w_up_a, w_up_b, proj, ga_col, gb_col, bm=1024, bn=512):
    n, ka = sgu.shape
    kb = att.shape[1]
    d = w_up_a.shape[1]
    gac, gbc = ga_col // bn, gb_col // bn
    return pl.pallas_call(
        _merge_kernel,
        grid=(n // bm, d // bn),
        in_specs=[
            pl.BlockSpec((bm, ka), lambda i, j: (i, 0)),
            pl.BlockSpec((bm, kb), lambda i, j: (i, 0)),
            pl.BlockSpec((ka, bn), lambda i, j: (0, j)),
            pl.BlockSpec((kb, bn), lambda i, j: (0, j)),
            pl.BlockSpec((bm, bn), lambda i, j: (i, gac + j)),
            pl.BlockSpec((bm, bn), lambda i, j: (i, gbc + j)),
        ],
        out_specs=pl.BlockSpec((bm, bn), lambda i, j: (i, j)),
        out_shape=jax.ShapeDtypeStruct((n, d), BF16),
        compiler_params=_cparams(("arbitrary", "arbitrary")),
        name="merge",
    )(sgu, att, w_up_a, w_up_b, proj, proj)


def _resid_kernel(a_ref, w_ref, x_ref, mod_ref, o_ref, *, gate_idx):
    y = _dot(a_ref[...], w_ref[...].astype(BF16))
    gate = mod_ref[0][gate_idx:gate_idx + 1]
    o_ref[...] = x_ref[...] + gate * y


def _resid_matmul(a, w, x2, mod, gate_idx, seq, name, bm=1024, bn=512):
    n, k = a.shape
    d = w.shape[1]
    return pl.pallas_call(
        functools.partial(_resid_kernel, gate_idx=gate_idx),
        grid=(n // bm, d // bn),
        in_specs=[
            pl.BlockSpec((bm, k), lambda i, j: (i, 0)),
            pl.BlockSpec((k, bn), lambda i, j: (0, j)),
            pl.BlockSpec((bm, bn), lambda i, j: (i, j)),
            pl.BlockSpec((1, N_MOD, bn), lambda i, j: (i * bm // seq, 0, j)),
        ],
        out_specs=pl.BlockSpec((bm, bn), lambda i, j: (i, j)),
        out_shape=jax.ShapeDtypeStruct((n, d), F32),
        compiler_params=_cparams(("arbitrary", "arbitrary")),
        name=name,
    )(a, w, x2, mod)


def _first_argmax(vals, iota, size):
    m = jnp.max(vals, axis=0, keepdims=True)
    idx = jnp.min(jnp.where(vals == m, iota, size), axis=0, keepdims=True)
    return m, idx


def _router_kernel(x_ref, g_ref, mod_ref, wr_ref, rb_ref, hp_ref, ids_ref, wts_ref, rank_ref, cnt_ref,
                   carry_ref):
    i = pl.program_id(0)
    bm = x_ref.shape[0]
    e = N_EXPERTS
    per = e // N_GROUPS

    @pl.when(i == 0)
    def _():
        carry_ref[...] = jnp.zeros_like(carry_ref)

    h = _norm_mod(x_ref[...], g_ref, mod_ref, 3, 4)
    hp_ref[...] = _pack_halves(h)

    h_hi = h.astype(BF16)
    h_lo = (h - h_hi.astype(F32)).astype(BF16)
    wr = wr_ref[...]
    w_hi = wr.astype(BF16)
    w_lo = (wr - w_hi.astype(F32)).astype(BF16)
    logits = _dot_nt(w_hi, h_hi) + (_dot_nt(w_hi, h_lo) + _dot_nt(w_lo, h_hi))
    scores = jax.nn.sigmoid(logits)
    choice = scores + rb_ref[...]

    iota_p = lax.broadcasted_iota(I32, (per, bm), 0)
    gs_rows = []
    for g in range(N_GROUPS):
        cg = choice[g * per:(g + 1) * per, :]
        m1, i1 = _first_argmax(cg, iota_p, per)
        m2 = jnp.max(jnp.where(iota_p == i1, -jnp.inf, cg), axis=0, keepdims=True)
        gs_rows.append(m1 + m2)
    gs = jnp.concatenate(gs_rows, axis=0)

    iota_g = lax.broadcasted_iota(I32, (N_GROUPS, bm), 0)
    gsel = jnp.zeros((N_GROUPS, bm), F32)
    for _ in range(TOPK_GROUPS):
        _, ig = _first_argmax(gs, iota_g, N_GROUPS)
        hit = iota_g == ig
        gsel = jnp.where(hit, 1.0, gsel)
        gs = jnp.where(hit, -jnp.inf, gs)
    emask = jnp.concatenate(
        [jnp.broadcast_to(gsel[g:g + 1, :], (per, bm)) for g in range(N_GROUPS)], axis=0)

    iota_e = lax.broadcasted_iota(I32, (e, bm), 0)
    masked = jnp.where(emask > 0.5, choice, -jnp.inf)
    self_f = jnp.zeros((e, bm), F32)
    idx_rows, w_rows = [], []
    for _ in range(TOP_K):
        _, ie = _first_argmax(masked, iota_e, e)
        hit = iota_e == ie
        self_f = jnp.where(hit, 1.0, self_f)
        idx_rows.append(ie)
        w_rows.append(jnp.sum(jnp.where(hit, scores, 0.0), axis=0, keepdims=True))
        masked = jnp.where(hit, -jnp.inf, masked)
    w = jnp.concatenate(w_rows, axis=0)
    w = w / jnp.sum(w, axis=0, keepdims=True) * ROUTED_SCALE

    t_row = lax.broadcasted_iota(I32, (bm, bm), 0)
    t_col = lax.broadcasted_iota(I32, (bm, bm), 1)
    before = jnp.where(t_row < t_col, 1.0, 0.0).astype(BF16)
    rank_mat = _dot(self_f.astype(BF16), before) + carry_ref[:, :1]
    rank_rows = [jnp.sum(jnp.where(iota_e == ie, rank_mat, 0.0), axis=0, keepdims=True)
                 for ie in idx_rows]
    carry_ref[...] = carry_ref[...] + jnp.sum(self_f, axis=1, keepdims=True)

    pad_i = jnp.zeros((K_PAD - TOP_K, bm), I32)
    ids_ref[...] = jnp.concatenate(idx_rows + [pad_i], axis=0)
    wts_ref[...] = jnp.concatenate([w, jnp.zeros((K_PAD - TOP_K, bm), F32)], axis=0)
    rank_ref[...] = jnp.concatenate([r.astype(I32) for r in rank_rows] + [pad_i], axis=0)
    cnt_ref[...] = carry_ref[...].astype(I32)


def _router(x1, g, mod, w_router, router_bias, seq, bm=512):
    n, d = x1.shape
    e = N_EXPERTS
    tok = lambda i: (0, i)
    return pl.pallas_call(
        _router_kernel,
        grid=(n // bm,),
        in_specs=[
            pl.BlockSpec((bm, d), lambda i: (i, 0)),
            pl.BlockSpec((1, d), lambda i: (0, 0)),
            pl.BlockSpec((1, N_MOD, d), lambda i: (i * bm // seq, 0, 0)),
            pl.BlockSpec((e, d), lambda i: (0, 0)),
            pl.BlockSpec((e, 1), lambda i: (0, 0)),
        ],
        out_specs=[
            pl.BlockSpec((bm, d // 2), lambda i: (i, 0)),
            pl.BlockSpec((K_PAD, bm), tok),
            pl.BlockSpec((K_PAD, bm), tok),
            pl.BlockSpec((K_PAD, bm), tok),
            pl.BlockSpec((e, 128), lambda i: (0, 0)),
        ],
        out_shape=[
            jax.ShapeDtypeStruct((n, d // 2), U32),
            jax.ShapeDtypeStruct((K_PAD, n), I32),
            jax.ShapeDtypeStruct((K_PAD, n), F32),
            jax.ShapeDtypeStruct((K_PAD, n), I32),
            jax.ShapeDtypeStruct((e, 128), I32),
        ],
        scratch_shapes=[pltpu.VMEM((e, 128), F32)],
        compiler_params=_cparams(("arbitrary",)),
        name="router",
    )(x1, g.reshape(1, d), mod, w_router.T, router_bias.reshape(e, 1))


def _pos_kernel(off_ref, ids_ref, rank_ref, o_ref):
    ids = ids_ref[...]
    pos = rank_ref[...]
    for ex in range(N_EXPERTS):
        pos = pos + jnp.where(ids == ex, off_ref[ex], 0)
    o_ref[...] = pos


def _positions(offsets, ids, ranks):
    return pl.pallas_call(
        _pos_kernel,
        in_specs=[
            pl.BlockSpec(memory_space=pltpu.SMEM),
            pl.BlockSpec(memory_space=pltpu.VMEM),
            pl.BlockSpec(memory_space=pltpu.VMEM),
        ],
        out_specs=pl.BlockSpec(memory_space=pltpu.VMEM),
        out_shape=jax.ShapeDtypeStruct(ids.shape, I32),
        name="positions",
    )(offsets, ids, ranks)


def _zero_fill_gaps(pad_ref, xs_ref, zeros, pad_sem, tm):
    def fill():
        zeros[...] = jnp.zeros_like(zeros)
        group_bits = [1 << b for b in reversed(range((tm // SUBLANES - 1).bit_length()))]

        def pad_copies(ex):
            start, gap = pad_ref[0, ex], pad_ref[1, ex]
            end = start + gap
            groups, singles = gap >> 3, gap & (SUBLANES - 1)
            out = []
            for j in range(SUBLANES - 1):
                out.append((j < singles, pltpu.make_async_copy(
                    zeros.at[pl.ds(0, 1)], xs_ref.at[pl.ds(start + j, 1)], pad_sem)))
            for bit in group_bits:
                first = pl.multiple_of(end - SUBLANES * (groups & (2 * bit - 1)), SUBLANES)
                out.append(((groups & bit) != 0, pltpu.make_async_copy(
                    zeros.at[pl.ds(0, SUBLANES * bit)], xs_ref.at[pl.ds(first, SUBLANES * bit)], pad_sem)))
            return out

        def issue_pad(ex, _):
            for cond, cp in pad_copies(ex):
                @pl.when(cond)
                def _():
                    cp.start()
            return 0

        def drain_pad(ex, _):
            for cond, cp in pad_copies(ex):
                @pl.when(cond)
                def _():
                    cp.wait()
            return 0

        lax.fori_loop(0, N_EXPERTS, issue_pad, 0)
        lax.fori_loop(0, N_EXPERTS, drain_pad, 0)

        zrows = zeros.shape[0]
        first = (pad_ref[0, N_EXPERTS - 1] + pad_ref[1, N_EXPERTS - 1]) // zrows

        def tail_copy(g):
            return pltpu.make_async_copy(zeros, xs_ref.at[pl.ds(pl.multiple_of(g * zrows, zrows), zrows)],
                                         pad_sem)

        def issue_tail(g, _):
            tail_copy(g).start()
            return 0

        def drain_tail(g, _):
            tail_copy(g).wait()
            return 0

        lax.fori_loop(first, xs_ref.shape[0] // zrows, issue_tail, 0)
        lax.fori_loop(first, xs_ref.shape[0] // zrows, drain_tail, 0)

    fill()


def _expert_tiles(ts_ref, src_ref, dst_ref, in_buf, out_buf, zero_buf, in_sem, out_sem, zero_sem, compute):
    e = pl.program_id(0)
    n_exp = pl.num_programs(0)
    n_in, tm = in_buf.shape[0], in_buf.shape[1]
    n_out = out_buf.shape[0]
    n_tiles = dst_ref.shape[0] // tm
    n_used = ts_ref[n_exp]

    def rows(g):
        return pl.ds(pl.multiple_of(g * tm, tm), tm)

    def load(g):
        return pltpu.make_async_copy(src_ref.at[rows(g)], in_buf.at[g % n_in], in_sem.at[g % n_in])

    def store(g):
        return pltpu.make_async_copy(out_buf.at[g % n_out], dst_ref.at[rows(g)], out_sem.at[g % n_out])

    @pl.when(e == 0)
    def _():
        for g in range(n_in - 1):
            @pl.when(g < n_used)
            def _():
                load(g).start(priority=TILE_DMA_PRIORITY)

    def tile(g, _):
        @pl.when(g + n_in - 1 < n_used)
        def _():
            load(g + n_in - 1).start(priority=TILE_DMA_PRIORITY)

        load(g).wait()

        @pl.when(g >= n_out)
        def _():
            store(g - n_out).wait()

        out_buf[g % n_out] = compute(in_buf[g % n_in])
        store(g).start(priority=TILE_DMA_PRIORITY)
        return 0

    lax.fori_loop(ts_ref[e], ts_ref[e + 1], tile, 0)

    @pl.when(e == n_exp - 1)
    def _():
        for back in range(n_out, 0, -1):
            @pl.when(n_used >= back)
            def _():
                store(n_used - back).wait()

        zero_buf[...] = jnp.zeros_like(zero_buf)

        def fill(g):
            return pltpu.make_async_copy(zero_buf, dst_ref.at[rows(g)], zero_sem)

        def start_fill(g, _):
            fill(g).start()
            return 0

        def wait_fill(g, _):
            fill(g).wait()
            return 0

        lax.fori_loop(n_used, n_tiles, start_fill, 0)
        lax.fori_loop(n_used, n_tiles, wait_fill, 0)


def _expert_up_kernel(ts_ref, wg_ref, wu_ref, xs_ref, act_ref, w_scr, in_buf, out_buf, zero_buf,
                      in_sem, out_sem, zero_sem):
    e = pl.program_id(0)
    f = wg_ref.shape[2]

    @pl.when(ts_ref[e + 1] > ts_ref[e])
    def _():
        w_scr[:, :f] = wg_ref[0].astype(BF16)
        w_scr[:, f:] = wu_ref[0].astype(BF16)

    def compute(xp):
        lo, hi = _unpack_halves(xp)
        x = jnp.concatenate([lo.astype(BF16), hi.astype(BF16)], axis=1)
        gu = _dot(x, w_scr[...])
        g, u = gu[:, :f], gu[:, f:]
        return (g * jax.nn.sigmoid(g) * u).astype(BF16)

    _expert_tiles(ts_ref, xs_ref, act_ref, in_buf, out_buf, zero_buf, in_sem, out_sem, zero_sem, compute)


def _expert_up(tile_start, xs, w_gate_e, w_up_e, tm, in_slots=4, out_slots=8):
    p, c = xs.shape
    e, d, f = w_gate_e.shape
    grid_spec = pltpu.PrefetchScalarGridSpec(
        num_scalar_prefetch=1,
        grid=(e,),
        in_specs=[
            pl.BlockSpec((1, d, f), lambda i, ts: (i, 0, 0)),
            pl.BlockSpec((1, d, f), lambda i, ts: (i, 0, 0)),
            pl.BlockSpec(memory_space=pl.ANY),
        ],
        out_specs=pl.BlockSpec(memory_space=pl.ANY),
        scratch_shapes=[
            pltpu.VMEM((d, 2 * f), BF16),
            pltpu.VMEM((in_slots, tm, c), U32),
            pltpu.VMEM((out_slots, tm, f), BF16),
            pltpu.VMEM((tm, f), BF16),
            pltpu.SemaphoreType.DMA((in_slots,)),
            pltpu.SemaphoreType.DMA((out_slots,)),
            pltpu.SemaphoreType.DMA(()),
        ],
    )
    return pl.pallas_call(
        _expert_up_kernel,
        grid_spec=grid_spec,
        out_shape=jax.ShapeDtypeStruct((p, f), BF16),
        compiler_params=_cparams(("arbitrary",)),
        name="expert_up",
    )(tile_start, w_gate_e, w_up_e, xs)


def _expert_down_kernel(ts_ref, wd_ref, act_ref, ys_ref, w_scr, in_buf, out_buf, zero_buf,
                        in_sem, out_sem, zero_sem):
    e = pl.program_id(0)

    @pl.when(ts_ref[e + 1] > ts_ref[e])
    def _():
        w_scr[...] = wd_ref[0].astype(BF16)

    def compute(a):
        return _pack_halves(_dot(a, w_scr[...]))

    _expert_tiles(ts_ref, act_ref, ys_ref, in_buf, out_buf, zero_buf, in_sem, out_sem, zero_sem, compute)


def _expert_down(tile_start, act, w_down_e, tm, in_slots=8, out_slots=8):
    p, f = act.shape
    e, _, d = w_down_e.shape
    grid_spec = pltpu.PrefetchScalarGridSpec(
        num_scalar_prefetch=1,
        grid=(e,),
        in_specs=[
            pl.BlockSpec((1, f, d), lambda i, ts: (i, 0, 0)),
            pl.BlockSpec(memory_space=pl.ANY),
        ],
        out_specs=pl.BlockSpec(memory_space=pl.ANY),
        scratch_shapes=[
            pltpu.VMEM((f, d), BF16),
            pltpu.VMEM((in_slots, tm, f), BF16),
            pltpu.VMEM((out_slots, tm, d // 2), U32),
            pltpu.VMEM((tm, d // 2), U32),
            pltpu.SemaphoreType.DMA((in_slots,)),
            pltpu.SemaphoreType.DMA((out_slots,)),
            pltpu.SemaphoreType.DMA(()),
        ],
    )
    return pl.pallas_call(
        _expert_down_kernel,
        grid_spec=grid_spec,
        out_shape=jax.ShapeDtypeStruct((p, d // 2), U32),
        compiler_params=_cparams(("arbitrary",)),
        name="expert_down",
    )(tile_start, w_down_e, act)


def _shared_dispatch_kernel(pad_ref, pos_ref, h_ref, wg_ref, wu_ref, o_ref, xs_ref, zeros, sem, pad_sem, *, tm):
    i, j = pl.program_id(0), pl.program_id(1)
    nj = pl.num_programs(1)
    step = i * nj + j
    bm, c = h_ref.shape
    share = bm // nj
    kc = c // SHARED_K_PIECES
    per_piece = share // (2 * SHARED_K_PIECES)
    assert per_piece * 2 * SHARED_K_PIECES == share

    @pl.when(step == 0)
    def _():
        _zero_fill_gaps(pad_ref, xs_ref, zeros, pad_sem, tm)

    def scatter_rows(first_row, n_rows, sem_slot):
        for r in range(first_row, first_row + n_rows):
            row = j * share + r
            for k in range(TOP_K):
                p = pos_ref[0, k, row]
                pltpu.make_async_copy(h_ref.at[pl.ds(row, 1)], xs_ref.at[pl.ds(p, 1)],
                                      sem.at[sem_slot]).start(priority=k % DMA_QUEUES)

    def wait_rows(sem_slot):
        for _ in range(TOP_K):
            pltpu.make_async_copy(h_ref.at[pl.ds(0, share)], xs_ref.at[pl.ds(0, share)], sem.at[sem_slot]).wait()

    slot = step % 2
    g = jnp.zeros(o_ref.shape, F32)
    u = jnp.zeros(o_ref.shape, F32)
    done = 0
    for piece in range(SHARED_K_PIECES):
        lo, hi = _unpack_halves(h_ref[:, piece * kc:(piece + 1) * kc])
        for half, xk in ((0, lo.astype(BF16)), (1, hi.astype(BF16))):
            k0 = half * c + piece * kc
            scatter_rows(done, per_piece, slot)
            done += per_piece
            g = g + _dot(xk, wg_ref[k0:k0 + kc, :].astype(BF16))
            u = u + _dot(xk, wu_ref[k0:k0 + kc, :].astype(BF16))
    o_ref[...] = (g * jax.nn.sigmoid(g) * u).astype(o_ref.dtype)

    @pl.when(j > 0)
    def _():
        wait_rows(1 - slot)

    @pl.when(j == nj - 1)
    def _():
        wait_rows(slot)


def _shared_up_dispatch(hp, w_gate_s, w_up_s, pos3, pads, n_rows, tm, bm=1024, bn=256):
    n, c = hp.shape
    f = w_gate_s.shape[1]
    return pl.pallas_call(
        functools.partial(_shared_dispatch_kernel, tm=tm),
        grid=(n // bm, f // bn),
        in_specs=[
            pl.BlockSpec(memory_space=pltpu.SMEM),
            pl.BlockSpec((1, K_PAD, bm), lambda i, j: (i, 0, 0), memory_space=pltpu.SMEM),
            pl.BlockSpec((bm, c), lambda i, j: (i, 0)),
            pl.BlockSpec((2 * c, bn), lambda i, j: (0, j)),
            pl.BlockSpec((2 * c, bn), lambda i, j: (0, j)),
        ],
        out_specs=[pl.BlockSpec((bm, bn), lambda i, j: (i, j)), pl.BlockSpec(memory_space=pl.ANY)],
        out_shape=[jax.ShapeDtypeStruct((n, f), BF16), jax.ShapeDtypeStruct((n_rows, c), U32)],
        scratch_shapes=[pltpu.VMEM((tm // 2, c), U32), pltpu.SemaphoreType.DMA((2,)),
                        pltpu.SemaphoreType.DMA(())],
        compiler_params=_cparams(("arbitrary", "arbitrary")),
        name="shared_up_dispatch",
    )(pads, pos3, hp, w_gate_s, w_up_s)


def _combine_kernel(pos_ref, pos_next_ref, x_ref, s_ref, wt_ref, mod_ref, g_ref, ys_ref, wd_ref, o_ref,
                    ybuf, wd_scr, stage, sem, wd_sem):
    i = pl.program_id(0)
    bt = x_ref.shape[0]
    slot = i % 2
    assert bt % SUBLANES == 0

    @pl.when(i == 0)
    def _():
        rows = stage.shape[0]
        for part in range(wd_scr.shape[0] // rows):
            cp = pltpu.make_async_copy(wd_ref.at[pl.ds(part * rows, rows)], stage, wd_sem)
            cp.start()
            cp.wait()
            wd_scr[part * rows:(part + 1) * rows, :] = stage[...].astype(BF16)

    def row_gather(p_ref, r, dst_slot):
        for k in range(TOP_K):
            p = p_ref[0, k, r]
            pltpu.make_async_copy(ys_ref.at[pl.ds(p, 1)], ybuf.at[dst_slot, k, pl.ds(r, 1)],
                                  sem.at[dst_slot]).start(priority=k % DMA_QUEUES)

    def wait_gather(dst_slot):
        for k in range(TOP_K):
            pltpu.make_async_copy(ys_ref.at[pl.ds(0, bt)], ybuf.at[dst_slot, k], sem.at[dst_slot]).wait()

    @pl.when(i == 0)
    def _():
        def first(r, _):
            row_gather(pos_ref, r, 0)
            return 0

        lax.fori_loop(0, bt, first, 0, unroll=ROW_DMA_UNROLL)

    wait_gather(slot)
    shared = _dot(s_ref[...], wd_scr[...])
    gate = mod_ref[0][5:6]
    c = ybuf.shape[3]
    for j in range(bt // SUBLANES):
        for r in range(j * SUBLANES, (j + 1) * SUBLANES):
            row_gather(pos_next_ref, r, 1 - slot)
        rs = slice(j * SUBLANES, (j + 1) * SUBLANES)
        acc_lo = jnp.zeros((SUBLANES, c), F32)
        acc_hi = jnp.zeros((SUBLANES, c), F32)
        for k in range(TOP_K):
            lo, hi = _unpack_halves(ybuf[slot, k, rs])
            wk = wt_ref[rs, k:k + 1]
            acc_lo = acc_lo + wk * lo
            acc_hi = acc_hi + wk * hi
        moe = jnp.concatenate([acc_lo, acc_hi], axis=1) + shared[rs]
        x = x_ref[rs, :] + gate * moe
        o_ref[rs, :] = x * lax.rsqrt(jnp.mean(x * x, axis=-1, keepdims=True) + EPS) * g_ref[...]

    @pl.when(i == pl.num_programs(0) - 1)
    def _():
        wait_gather(1 - slot)


def _combine(pos3, x1, sact, wts_t, mod, final_g, ys, w_down_s, seq, bt, stage_rows=128):
    n, d = x1.shape
    f = sact.shape[1]
    nb = n // bt
    pos_spec = lambda imap: pl.BlockSpec((1, K_PAD, bt), imap, memory_space=pltpu.SMEM)
    return pl.pallas_call(
        _combine_kernel,
        grid=(nb,),
        in_specs=[
            pos_spec(lambda i: (i, 0, 0)),
            pos_spec(lambda i: (jnp.minimum(i + 1, nb - 1), 0, 0)),
            pl.BlockSpec((bt, d), lambda i: (i, 0)),
            pl.BlockSpec((bt, f), lambda i: (i, 0)),
            pl.BlockSpec((bt, K_PAD), lambda i: (i, 0)),
            pl.BlockSpec((1, N_MOD, d), lambda i: (i * bt // seq, 0, 0)),
            pl.BlockSpec((1, d), lambda i: (0, 0)),
            pl.BlockSpec(memory_space=pl.ANY),
            pl.BlockSpec(memory_space=pl.ANY),
        ],
        out_specs=pl.BlockSpec((bt, d), lambda i: (i, 0)),
        out_shape=jax.ShapeDtypeStruct((n, d), F32),
        scratch_shapes=[
            pltpu.VMEM((2, TOP_K, bt, d // 2), U32),
            pltpu.VMEM((f, d), BF16),
            pltpu.VMEM((stage_rows, d), F32),
            pltpu.SemaphoreType.DMA((2,)),
            pltpu.SemaphoreType.DMA(()),
        ],
        compiler_params=_cparams(("arbitrary",)),
        name="combine",
    )(pos3, pos3, x1, sact, wts_t, mod, final_g.reshape(1, d), ys, w_down_s)


def _moe_tiles(counts, n_tokens, tm):
    e = counts.shape[0]
    n_tiles = n_tokens * TOP_K // tm + e
    tiles_per = (counts + tm - 1) // tm
    tile_end = jnp.cumsum(tiles_per)
    tile_start = jnp.concatenate([jnp.zeros((1,), I32), tile_end.astype(I32)])
    offsets = tile_start[:-1] * tm
    pads = jnp.stack([offsets + counts, tiles_per * tm - counts]).astype(I32)
    return offsets, tile_start, pads, n_tiles


def kernel(x, c, w_ada, b_ada, norm1_g, w_in, w_s, b_s, w_up_a, w_up_b, w_o, norm2_g, w_router,
           router_bias, w_gate_e, w_up_e, w_down_e, w_gate_s, w_up_s, w_down_s, final_g):
    batch, seq, d = x.shape
    n = batch * seq
    depth = w_ada.shape[0]
    a_width = w_up_a.shape[1]
    b_width = w_up_b.shape[1]
    n_heads = b_width // HEAD_DIM
    tm = 256
    bm_dispatch = 1024
    bt_combine = 128

    assert depth == 1, "the combine kernel applies the final norm, so only one layer is supported"
    l = 0
    x2 = x.reshape(n, d)
    mod = _ada(c, w_ada[l], b_ada[l])
    h1 = _norm1(x2, norm1_g[l], mod, seq)
    proj = _inproj(h1, w_in[l], a_width, b_width)
    sgu = _sgu(proj, w_s[l], b_s[l], a_width)
    att = _attention(proj, batch, seq, 2 * a_width, 2 * a_width + b_width,
                     2 * a_width + 2 * b_width, n_heads)
    merged = _merge(sgu, att, w_up_a[l], w_up_b[l], proj,
                    2 * a_width + 3 * b_width, 2 * a_width + 3 * b_width + d)
    x1 = _resid_matmul(merged, w_o[l], x2, mod, 2, seq, "out_proj")

    h2p, ids, wts, ranks, counts = _router(x1, norm2_g[l], mod, w_router[l], router_bias[l], seq)
    offsets, tile_start, pads, n_tiles = _moe_tiles(counts[:, 0], n, tm)
    pos = _positions(offsets, ids, ranks)
    blocked = lambda b: pos.reshape(K_PAD, n // b, b).transpose(1, 0, 2)
    sact, xs = _shared_up_dispatch(h2p, w_gate_s[l], w_up_s[l], blocked(bm_dispatch), pads, n_tiles * tm, tm,
                                   bm=bm_dispatch)
    act = _expert_up(tile_start, xs, w_gate_e[l], w_up_e[l], tm)
    ys = _expert_down(tile_start, act, w_down_e[l], tm)
    out = _combine(blocked(bt_combine), x1, sact, wts.T, mod, final_g, ys, w_down_s[l], seq, bt_combine)
    return out.reshape(batch, seq, d)
```

```python
import functools

import jax
import jax.numpy as jnp
from jax import lax
from jax.experimental import pallas as pl
from jax.experimental.pallas import tpu as pltpu

F32 = jnp.float32
BF16 = jnp.bfloat16
I32 = jnp.int32
U32 = jnp.uint32

EPS = 1e-6
A_GROUPS = 8
A_CHUNK = 128
HEAD_DIM = 128
N_EXPERTS = 64
TOP_K = 6
N_GROUPS = 8
TOPK_GROUPS = 4
ROUTED_SCALE = 2.5
N_MOD = 6
SUBLANES = 8
MXU_COLS = 256
K_PAD = SUBLANES
SHARED_K_PIECES = 4
ROW_DMA_UNROLL = 8
DMA_QUEUES = 2
TILE_DMA_PRIORITY = 1
DEAD_LOG = -105.0

VMEM_LIMIT = 56 * 1024 * 1024


def _cparams(sem):
    return pltpu.CompilerParams(dimension_semantics=sem, vmem_limit_bytes=VMEM_LIMIT)


def _dot(a, b):
    return jnp.dot(a, b, preferred_element_type=F32)


def _dot_nt(a, b):
    return lax.dot_general(a, b, (((1,), (1,)), ((), ())), preferred_element_type=F32)


def _pack_halves(h):
    c = h.shape[1] // 2
    lo = pltpu.bitcast(h[:, :c].astype(BF16).astype(F32), U32)
    hi = pltpu.bitcast(h[:, c:].astype(BF16).astype(F32), U32)
    return (hi & jnp.uint32(0xFFFF0000)) | (lo >> 16)


def _unpack_halves(p):
    lo = pltpu.bitcast(p << 16, F32)
    hi = pltpu.bitcast(p & jnp.uint32(0xFFFF0000), F32)
    return lo, hi


def _ada_kernel(c_ref, w_ref, b_ref, o_ref):
    c = c_ref[...]
    cact = (c * jax.nn.sigmoid(c)).astype(BF16)
    o_ref[...] = _dot(cact, w_ref[...].astype(BF16)) + b_ref[...]


def _ada(c, w_ada, b_ada, bn=512):
    b, d = c.shape
    n = w_ada.shape[1]
    cp = jnp.zeros((8, d), F32).at[:b].set(c)
    out = pl.pallas_call(
        _ada_kernel,
        grid=(n // bn,),
        in_specs=[
            pl.BlockSpec((8, d), lambda j: (0, 0)),
            pl.BlockSpec((d, bn), lambda j: (0, j)),
            pl.BlockSpec((1, bn), lambda j: (0, j)),
        ],
        out_specs=pl.BlockSpec((8, bn), lambda j: (0, j)),
        out_shape=jax.ShapeDtypeStruct((8, n), F32),
        compiler_params=_cparams(("arbitrary",)),
        name="ada",
    )(cp, w_ada, b_ada.reshape(1, n))
    return out[:b].reshape(b, N_MOD, d)


def _norm_mod(x, g_ref, mod_ref, shift_idx, scale_idx):
    y = x * lax.rsqrt(jnp.mean(x * x, axis=-1, keepdims=True) + EPS) * g_ref[...]
    mod = mod_ref[0]
    return y * (1.0 + mod[scale_idx:scale_idx + 1]) + mod[shift_idx:shift_idx + 1]


def _norm1_kernel(x_ref, g_ref, mod_ref, o_ref):
    o_ref[...] = _norm_mod(x_ref[...], g_ref, mod_ref, 0, 1).astype(BF16)


def _norm1(x2, g, mod, seq, bm=512):
    n, d = x2.shape
    return pl.pallas_call(
        _norm1_kernel,
        grid=(n // bm,),
        in_specs=[
            pl.BlockSpec((bm, d), lambda i: (i, 0)),
            pl.BlockSpec((1, d), lambda i: (0, 0)),
            pl.BlockSpec((1, N_MOD, d), lambda i: (i * bm // seq, 0, 0)),
        ],
        out_specs=pl.BlockSpec((bm, d), lambda i: (i, 0)),
        out_shape=jax.ShapeDtypeStruct((n, d), BF16),
        compiler_params=_cparams(("arbitrary",)),
        name="norm1",
    )(x2, g.reshape(1, d), mod)


def _inproj_kernel(h_ref, w_ref, o_ref, *, n_gelu, n_ln, n_plain, group):
    j = pl.program_id(1)

    def run(epilogue):
        for s in range(o_ref.shape[1] // group):
            cs = slice(s * group, (s + 1) * group)
            acc = _dot(h_ref[...], w_ref[:, cs].astype(BF16))
            o_ref[:, cs] = epilogue(acc).astype(o_ref.dtype)

    def group_norm(acc):
        a = jax.nn.gelu(acc)
        cen = a - jnp.mean(a, axis=-1, keepdims=True)
        var = jnp.mean(cen * cen, axis=-1, keepdims=True)
        return cen * lax.rsqrt(var + EPS)

    @pl.when(j < n_gelu)
    def _():
        run(jax.nn.gelu)

    @pl.when((j >= n_gelu) & (j < n_ln))
    def _():
        run(group_norm)

    @pl.when((j >= n_ln) & (j < n_plain))
    def _():
        run(lambda acc: acc)

    @pl.when(j >= n_plain)
    def _():
        run(jax.nn.sigmoid)


def _inproj(h, w_in, a_width, b_width, bm=1024, bn=512):
    n, d = h.shape
    cols = w_in.shape[1]
    group = a_width // A_GROUPS
    kern = functools.partial(
        _inproj_kernel,
        n_gelu=a_width // bn,
        n_ln=2 * a_width // bn,
        n_plain=(2 * a_width + 3 * b_width) // bn,
        group=group,
    )
    return pl.pallas_call(
        kern,
        grid=(n // bm, cols // bn),
        in_specs=[
            pl.BlockSpec((bm, d), lambda i, j: (i, 0)),
            pl.BlockSpec((d, bn), lambda i, j: (0, j)),
        ],
        out_specs=pl.BlockSpec((bm, bn), lambda i, j: (i, j)),
        out_shape=jax.ShapeDtypeStruct((n, cols), BF16),
        compiler_params=_cparams(("arbitrary", "arbitrary")),
        name="inproj",
    )(h, w_in)


def _sgu_kernel(u_ref, v_ref, ws_ref, bs_ref, o_ref, *, group):
    t = A_CHUNK
    row = lax.broadcasted_iota(I32, (t, t), 0)
    col = lax.broadcasted_iota(I32, (t, t), 1)
    for g in range(A_GROUPS):
        ws = jnp.where(row >= col, ws_ref[g], 0.0).astype(BF16)
        bias = bs_ref[:, g:g + 1]
        for c in range(u_ref.shape[0] // t):
            rs = slice(c * t, (c + 1) * t)
            cs = slice(g * group, (g + 1) * group)
            mixed = _dot(ws, v_ref[rs, cs]) + bias
            o_ref[rs, cs] = (u_ref[rs, cs].astype(F32) * mixed).astype(o_ref.dtype)


def _sgu(proj, w_s, b_s, a_width, bm=512):
    n = proj.shape[0]
    group = a_width // A_GROUPS
    return pl.pallas_call(
        functools.partial(_sgu_kernel, group=group),
        grid=(n // bm,),
        in_specs=[
            pl.BlockSpec((bm, a_width), lambda i: (i, 0)),
            pl.BlockSpec((bm, a_width), lambda i: (i, 1)),
            pl.BlockSpec((A_GROUPS, A_CHUNK, A_CHUNK), lambda i: (0, 0, 0)),
            pl.BlockSpec((A_CHUNK, A_GROUPS), lambda i: (0, 0)),
        ],
        out_specs=pl.BlockSpec((bm, a_width), lambda i: (i, 0)),
        out_shape=jax.ShapeDtypeStruct((n, a_width), BF16),
        compiler_params=_cparams(("arbitrary",)),
        name="sgu",
    )(proj, proj, w_s, b_s.T)


def _attn_kernel(q_ref, k_ref, v_ref, o_ref, *, blk, heads, scale):
    seq = q_ref.shape[0]
    row = lax.broadcasted_iota(I32, (blk, blk), 0)
    col = lax.broadcasted_iota(I32, (blk, blk), 1)
    upper = jnp.where(row > col, 1.0, 0.0).astype(BF16)
    strict = col < row

    hslices = [slice(h * HEAD_DIM, (h + 1) * HEAD_DIM) for h in range(heads)]

    def update(qv, ks, cs, accs, diagonal):
        zs = [_dot_nt(qv[h], k_ref[pl.ds(ks, blk), hslices[h]]) * scale for h in range(heads)]
        sps = [jnp.maximum(z, 0.0) + jnp.log(1.0 + jnp.exp(-jnp.abs(z))) for z in zs]
        stays = [jnp.where(strict, -sp, 0.0) if diagonal else -sp for sp in sps]
        his = [s.astype(BF16) for s in stays]
        los = [(s - hi.astype(F32)).astype(BF16) for s, hi in zip(stays, his)]
        laters = [_dot(hi, upper) + _dot(lo, upper) for hi, lo in zip(his, los)]
        ws = [jnp.exp((zs[h] - sps[h]) + laters[h] + cs[h]) for h in range(heads)]
        if diagonal:
            ws = [jnp.where(strict, w, 0.0) for w in ws]
        accs = tuple(accs[h] + _dot(ws[h].astype(BF16), v_ref[pl.ds(ks, blk), hslices[h]])
                     for h in range(heads))
        cs = tuple(cs[h] + laters[h][:, :1] + stays[h][:, :1] for h in range(heads))
        return cs, accs

    def q_block(qi, _):
        qs = pl.multiple_of(qi * blk, blk)
        qv = [q_ref[pl.ds(qs, blk), hs] for hs in hslices]
        cs, accs = update(qv, qs, (jnp.zeros((blk, 1), F32),) * heads,
                          (jnp.zeros((blk, HEAD_DIM), F32),) * heads, True)

        def live(carry):
            n, cs, _ = carry
            cmax = jnp.max(cs[0])
            for c in cs[1:]:
                cmax = jnp.maximum(cmax, jnp.max(c))
            return (n <= qi) & (cmax > DEAD_LOG)

        def older_block(carry):
            n, cs, accs = carry
            ks = pl.multiple_of((qi - n) * blk, blk)
            cs, accs = update(qv, ks, cs, accs, False)
            return n + 1, cs, accs

        _, _, accs = lax.while_loop(live, older_block, (jnp.int32(1), cs, accs))
        for h in range(heads):
            o_ref[pl.ds(qs, blk), hslices[h]] = accs[h].astype(o_ref.dtype)
        return 0

    lax.fori_loop(0, seq // blk, q_block, 0)


def _attention(proj, batch, seq, q_col, k_col, v_col, n_heads, blk=256, heads=4):
    width = heads * HEAD_DIM
    qc, kc, vc = q_col // width, k_col // width, v_col // width
    return pl.pallas_call(
        functools.partial(_attn_kernel, blk=blk, heads=heads, scale=HEAD_DIM ** -0.5),
        grid=(batch, n_heads // heads),
        in_specs=[
            pl.BlockSpec((seq, width), lambda b, h: (b, qc + h)),
            pl.BlockSpec((seq, width), lambda b, h: (b, kc + h)),
            pl.BlockSpec((seq, width), lambda b, h: (b, vc + h)),
        ],
        out_specs=pl.BlockSpec((seq, width), lambda b, h: (b, h)),
        out_shape=jax.ShapeDtypeStruct((batch * seq, n_heads * HEAD_DIM), BF16),
        compiler_params=_cparams(("arbitrary", "arbitrary")),
        name="attn",
    )(proj, proj, proj)


def _merge_kernel(a_ref, b_ref, wa_ref, wb_ref, ga_ref, gb_ref, o_ref):
    for s in range(o_ref.shape[1] // MXU_COLS):
        cs = slice(s * MXU_COLS, (s + 1) * MXU_COLS)
        ya = _dot(a_ref[...], wa_ref[:, cs].astype(BF16))
        yb = _dot(b_ref[...], wb_ref[:, cs].astype(BF16))
        o_ref[:, cs] = (ga_ref[:, cs].astype(F32) * ya + gb_ref[:, cs].astype(F32) * yb).astype(o_ref.dtype)


def _merge(sgu, att, w_up_a, w_up_b, proj, ga_col, gb_col, bm=1024, bn=512):
    n, ka = sgu.shape
    kb = att.shape[1]
    d = w_up_a.shape[1]
    gac, gbc = ga_col // bn, gb_col // bn
    return pl.pallas_call(
        _merge_kernel,
        grid=(n // bm, d // bn),
        in_specs=[
            pl.BlockSpec((bm, ka), lambda i, j: (i, 0)),
            pl.BlockSpec((bm, kb), lambda i, j: (i, 0)),
            pl.BlockSpec((ka, bn), lambda i, j: (0, j)),
            pl.BlockSpec((kb, bn), lambda i, j: (0, j)),
            pl.BlockSpec((bm, bn), lambda i, j: (i, gac + j)),
            pl.BlockSpec((bm, bn), lambda i, j: (i, gbc + j)),
        ],
        out_specs=pl.BlockSpec((bm, bn), lambda i, j: (i, j)),
        out_shape=jax.ShapeDtypeStruct((n, d), BF16),
        compiler_params=_cparams(("arbitrary", "arbitrary")),
        name="merge",
    )(sgu, att, w_up_a, w_up_b, proj, proj)


def _resid_kernel(a_ref, w_ref, x_ref, mod_ref, o_ref, *, gate_idx):
    gate = mod_ref[0][gate_idx:gate_idx + 1]
    for s in range(o_ref.shape[1] // MXU_COLS):
        cs = slice(s * MXU_COLS, (s + 1) * MXU_COLS)
        y = _dot(a_ref[...], w_ref[:, cs].astype(BF16))
        o_ref[:, cs] = x_ref[:, cs] + gate[:, cs] * y


def _resid_matmul(a, w, x2, mod, gate_idx, seq, name, bm=1024, bn=512):
    n, k = a.shape
    d = w.shape[1]
    return pl.pallas_call(
        functools.partial(_resid_kernel, gate_idx=gate_idx),
        grid=(n // bm, d // bn),
        in_specs=[
            pl.BlockSpec((bm, k), lambda i, j: (i, 0)),
            pl.BlockSpec((k, bn), lambda i, j: (0, j)),
            pl.BlockSpec((bm, bn), lambda i, j: (i, j)),
            pl.BlockSpec((1, N_MOD, bn), lambda i, j: (i * bm // seq, 0, j)),
        ],
        out_specs=pl.BlockSpec((bm, bn), lambda i, j: (i, j)),
        out_shape=jax.ShapeDtypeStruct((n, d), F32),
        compiler_params=_cparams(("arbitrary", "arbitrary")),
        name=name,
    )(a, w, x2, mod)


def _first_argmax(vals, iota, size):
    m = jnp.max(vals, axis=0, keepdims=True)
    idx = jnp.min(jnp.where(vals == m, iota, size), axis=0, keepdims=True)
    return m, idx


def _router_kernel(x_ref, g_ref, mod_ref, wr_ref, rb_ref, hp_ref, ids_ref, wts_ref, rank_ref, cnt_ref,
                   carry_ref):
    i = pl.program_id(0)
    bm = x_ref.shape[0]
    e = N_EXPERTS
    per = e // N_GROUPS

    @pl.when(i == 0)
    def _():
        carry_ref[...] = jnp.zeros_like(carry_ref)

    h = _norm_mod(x_ref[...], g_ref, mod_ref, 3, 4)
    hp_ref[...] = _pack_halves(h)

    h_hi = h.astype(BF16)
    h_lo = (h - h_hi.astype(F32)).astype(BF16)
    wr = wr_ref[...]
    w_hi = wr.astype(BF16)
    w_lo = (wr - w_hi.astype(F32)).astype(BF16)
    logits = _dot_nt(w_hi, h_hi) + (_dot_nt(w_hi, h_lo) + _dot_nt(w_lo, h_hi))
    scores = jax.nn.sigmoid(logits)
    choice = scores + rb_ref[...]

    iota_p = lax.broadcasted_iota(I32, (per, bm), 0)
    gs_rows = []
    for g in range(N_GROUPS):
        cg = choice[g * per:(g + 1) * per, :]
        m1, i1 = _first_argmax(cg, iota_p, per)
        m2 = jnp.max(jnp.where(iota_p == i1, -jnp.inf, cg), axis=0, keepdims=True)
        gs_rows.append(m1 + m2)
    gs = jnp.concatenate(gs_rows, axis=0)

    iota_g = lax.broadcasted_iota(I32, (N_GROUPS, bm), 0)
    gsel = jnp.zeros((N_GROUPS, bm), F32)
    for _ in range(TOPK_GROUPS):
        _, ig = _first_argmax(gs, iota_g, N_GROUPS)
        hit = iota_g == ig
        gsel = jnp.where(hit, 1.0, gsel)
        gs = jnp.where(hit, -jnp.inf, gs)
    emask = jnp.concatenate(
        [jnp.broadcast_to(gsel[g:g + 1, :], (per, bm)) for g in range(N_GROUPS)], axis=0)

    iota_e = lax.broadcasted_iota(I32, (e, bm), 0)
    masked = jnp.where(emask > 0.5, choice, -jnp.inf)
    self_f = jnp.zeros((e, bm), F32)
    idx_rows, w_rows = [], []
    for _ in range(TOP_K):
        _, ie = _first_argmax(masked, iota_e, e)
        hit = iota_e == ie
        self_f = jnp.where(hit, 1.0, self_f)
        idx_rows.append(ie)
        w_rows.append(jnp.sum(jnp.where(hit, scores, 0.0), axis=0, keepdims=True))
        masked = jnp.where(hit, -jnp.inf, masked)
    w = jnp.concatenate(w_rows, axis=0)
    w = w / jnp.sum(w, axis=0, keepdims=True) * ROUTED_SCALE

    t_row = lax.broadcasted_iota(I32, (bm, bm), 0)
    t_col = lax.broadcasted_iota(I32, (bm, bm), 1)
    before = jnp.where(t_row < t_col, 1.0, 0.0).astype(BF16)
    rank_mat = _dot(self_f.astype(BF16), before) + carry_ref[:, :1]
    rank_rows = [jnp.sum(jnp.where(iota_e == ie, rank_mat, 0.0), axis=0, keepdims=True)
                 for ie in idx_rows]
    carry_ref[...] = carry_ref[...] + jnp.sum(self_f, axis=1, keepdims=True)

    pad_i = jnp.zeros((K_PAD - TOP_K, bm), I32)
    ids_ref[...] = jnp.concatenate(idx_rows + [pad_i], axis=0)
    wts_ref[...] = jnp.concatenate([w, jnp.zeros((K_PAD - TOP_K, bm), F32)], axis=0)
    rank_ref[...] = jnp.concatenate([r.astype(I32) for r in rank_rows] + [pad_i], axis=0)
    cnt_ref[...] = carry_ref[...].astype(I32)


def _router(x1, g, mod, w_router, router_bias, seq, bm=512):
    n, d = x1.shape
    e = N_EXPERTS
    tok = lambda i: (0, i)
    return pl.pallas_call(
        _router_kernel,
        grid=(n // bm,),
        in_specs=[
            pl.BlockSpec((bm, d), lambda i: (i, 0)),
            pl.BlockSpec((1, d), lambda i: (0, 0)),
            pl.BlockSpec((1, N_MOD, d), lambda i: (i * bm // seq, 0, 0)),
            pl.BlockSpec((e, d), lambda i: (0, 0)),
            pl.BlockSpec((e, 1), lambda i: (0, 0)),
        ],
        out_specs=[
            pl.BlockSpec((bm, d // 2), lambda i: (i, 0)),
            pl.BlockSpec((K_PAD, bm), tok),
            pl.BlockSpec((K_PAD, bm), tok),
            pl.BlockSpec((K_PAD, bm), tok),
            pl.BlockSpec((e, 128), lambda i: (0, 0)),
        ],
        out_shape=[
            jax.ShapeDtypeStruct((n, d // 2), U32),
            jax.ShapeDtypeStruct((K_PAD, n), I32),
            jax.ShapeDtypeStruct((K_PAD, n), F32),
            jax.ShapeDtypeStruct((K_PAD, n), I32),
            jax.ShapeDtypeStruct((e, 128), I32),
        ],
        scratch_shapes=[pltpu.VMEM((e, 128), F32)],
        compiler_params=_cparams(("arbitrary",)),
        name="router",
    )(x1, g.reshape(1, d), mod, w_router.T, router_bias.reshape(e, 1))


def _pos_kernel(off_ref, ids_ref, rank_ref, o_ref):
    ids = ids_ref[...]
    pos = rank_ref[...]
    for ex in range(N_EXPERTS):
        pos = pos + jnp.where(ids == ex, off_ref[ex], 0)
    o_ref[...] = pos


def _positions(offsets, ids, ranks):
    return pl.pallas_call(
        _pos_kernel,
        in_specs=[
            pl.BlockSpec(memory_space=pltpu.SMEM),
            pl.BlockSpec(memory_space=pltpu.VMEM),
            pl.BlockSpec(memory_space=pltpu.VMEM),
        ],
        out_specs=pl.BlockSpec(memory_space=pltpu.VMEM),
        out_shape=jax.ShapeDtypeStruct(ids.shape, I32),
        name="positions",
    )(offsets, ids, ranks)


def _zero_fill_gaps(pad_ref, xs_ref, zeros, pad_sem, tm):
    def fill():
        zeros[...] = jnp.zeros_like(zeros)
        group_bits = [1 << b for b in reversed(range((tm // SUBLANES - 1).bit_length()))]

        def pad_copies(ex):
            start, gap = pad_ref[0, ex], pad_ref[1, ex]
            end = start + gap
            groups, singles = gap >> 3, gap & (SUBLANES - 1)
            out = []
            for j in range(SUBLANES - 1):
                out.append((j < singles, pltpu.make_async_copy(
                    zeros.at[pl.ds(0, 1)], xs_ref.at[pl.ds(start + j, 1)], pad_sem)))
            for bit in group_bits:
                first = pl.multiple_of(end - SUBLANES * (groups & (2 * bit - 1)), SUBLANES)
                out.append(((groups & bit) != 0, pltpu.make_async_copy(
                    zeros.at[pl.ds(0, SUBLANES * bit)], xs_ref.at[pl.ds(first, SUBLANES * bit)], pad_sem)))
            return out

        def issue_pad(ex, _):
            for cond, cp in pad_copies(ex):
                @pl.when(cond)
                def _():
                    cp.start()
            return 0

        def drain_pad(ex, _):
            for cond, cp in pad_copies(ex):
                @pl.when(cond)
                def _():
                    cp.wait()
            return 0

        lax.fori_loop(0, N_EXPERTS, issue_pad, 0)
        lax.fori_loop(0, N_EXPERTS, drain_pad, 0)

        zrows = zeros.shape[0]
        first = (pad_ref[0, N_EXPERTS - 1] + pad_ref[1, N_EXPERTS - 1]) // zrows

        def tail_copy(g):
            return pltpu.make_async_copy(zeros, xs_ref.at[pl.ds(pl.multiple_of(g * zrows, zrows), zrows)],
                                         pad_sem)

        def issue_tail(g, _):
            tail_copy(g).start()
            return 0

        def drain_tail(g, _):
            tail_copy(g).wait()
            return 0

        lax.fori_loop(first, xs_ref.shape[0] // zrows, issue_tail, 0)
        lax.fori_loop(first, xs_ref.shape[0] // zrows, drain_tail, 0)

    fill()


def _expert_tiles(ts_ref, src_ref, dst_ref, in_buf, out_buf, zero_buf, in_sem, out_sem, zero_sem, compute):
    e = pl.program_id(0)
    n_exp = pl.num_programs(0)
    n_in, tm = in_buf.shape[0], in_buf.shape[1]
    n_out = out_buf.shape[0]
    n_tiles = dst_ref.shape[0] // tm
    n_used = ts_ref[n_exp]

    def rows(g):
        return pl.ds(pl.multiple_of(g * tm, tm), tm)

    def load(g):
        return pltpu.make_async_copy(src_ref.at[rows(g)], in_buf.at[g % n_in], in_sem.at[g % n_in])

    def store(g):
        return pltpu.make_async_copy(out_buf.at[g % n_out], dst_ref.at[rows(g)], out_sem.at[g % n_out])

    @pl.when(e == 0)
    def _():
        for g in range(n_in - 1):
            @pl.when(g < n_used)
            def _():
                load(g).start(priority=TILE_DMA_PRIORITY)

    def tile(g, _):
        @pl.when(g + n_in - 1 < n_used)
        def _():
            load(g + n_in - 1).start(priority=TILE_DMA_PRIORITY)

        load(g).wait()

        @pl.when(g >= n_out)
        def _():
            store(g - n_out).wait()

        out_buf[g % n_out] = compute(in_buf[g % n_in])
        store(g).start(priority=TILE_DMA_PRIORITY)
        return 0

    lax.fori_loop(ts_ref[e], ts_ref[e + 1], tile, 0)

    @pl.when(e == n_exp - 1)
    def _():
        for back in range(n_out, 0, -1):
            @pl.when(n_used >= back)
            def _():
                store(n_used - back).wait()

        zero_buf[...] = jnp.zeros_like(zero_buf)

        def fill(g):
            return pltpu.make_async_copy(zero_buf, dst_ref.at[rows(g)], zero_sem)

        def start_fill(g, _):
            fill(g).start()
            return 0

        def wait_fill(g, _):
            fill(g).wait()
            return 0

        lax.fori_loop(n_used, n_tiles, start_fill, 0)
        lax.fori_loop(n_used, n_tiles, wait_fill, 0)


def _expert_up_kernel(ts_ref, wg_ref, wu_ref, xs_ref, act_ref, w_scr, in_buf, out_buf, zero_buf,
                      in_sem, out_sem, zero_sem):
    e = pl.program_id(0)
    f = wg_ref.shape[2]

    @pl.when(ts_ref[e + 1] > ts_ref[e])
    def _():
        w_scr[:, :f] = wg_ref[0].astype(BF16)
        w_scr[:, f:] = wu_ref[0].astype(BF16)

    def compute(xp):
        lo, hi = _unpack_halves(xp)
        x = jnp.concatenate([lo.astype(BF16), hi.astype(BF16)], axis=1)
        gu = _dot(x, w_scr[...])
        g, u = gu[:, :f], gu[:, f:]
        return (g * jax.nn.sigmoid(g) * u).astype(BF16)

    _expert_tiles(ts_ref, xs_ref, act_ref, in_buf, out_buf, zero_buf, in_sem, out_sem, zero_sem, compute)


def _expert_up(tile_start, xs, w_gate_e, w_up_e, tm, in_slots=4, out_slots=8):
    p, c = xs.shape
    e, d, f = w_gate_e.shape
    grid_spec = pltpu.PrefetchScalarGridSpec(
        num_scalar_prefetch=1,
        grid=(e,),
        in_specs=[
            pl.BlockSpec((1, d, f), lambda i, ts: (i, 0, 0)),
            pl.BlockSpec((1, d, f), lambda i, ts: (i, 0, 0)),
            pl.BlockSpec(memory_space=pl.ANY),
        ],
        out_specs=pl.BlockSpec(memory_space=pl.ANY),
        scratch_shapes=[
            pltpu.VMEM((d, 2 * f), BF16),
            pltpu.VMEM((in_slots, tm, c), U32),
            pltpu.VMEM((out_slots, tm, f), BF16),
            pltpu.VMEM((tm, f), BF16),
            pltpu.SemaphoreType.DMA((in_slots,)),
            pltpu.SemaphoreType.DMA((out_slots,)),
            pltpu.SemaphoreType.DMA(()),
        ],
    )
    return pl.pallas_call(
        _expert_up_kernel,
        grid_spec=grid_spec,
        out_shape=jax.ShapeDtypeStruct((p, f), BF16),
        compiler_params=_cparams(("arbitrary",)),
        name="expert_up",
    )(tile_start, w_gate_e, w_up_e, xs)


def _expert_down_kernel(ts_ref, wd_ref, act_ref, ys_ref, w_scr, in_buf, out_buf, zero_buf,
                        in_sem, out_sem, zero_sem):
    e = pl.program_id(0)

    @pl.when(ts_ref[e + 1] > ts_ref[e])
    def _():
        w_scr[...] = wd_ref[0].astype(BF16)

    def compute(a):
        return _pack_halves(_dot(a, w_scr[...]))

    _expert_tiles(ts_ref, act_ref, ys_ref, in_buf, out_buf, zero_buf, in_sem, out_sem, zero_sem, compute)


def _expert_down(tile_start, act, w_down_e, tm, in_slots=8, out_slots=8):
    p, f = act.shape
    e, _, d = w_down_e.shape
    grid_spec = pltpu.PrefetchScalarGridSpec(
        num_scalar_prefetch=1,
        grid=(e,),
        in_specs=[
            pl.BlockSpec((1, f, d), lambda i, ts: (i, 0, 0)),
            pl.BlockSpec(memory_space=pl.ANY),
        ],
        out_specs=pl.BlockSpec(memory_space=pl.ANY),
        scratch_shapes=[
            pltpu.VMEM((f, d), BF16),
            pltpu.VMEM((in_slots, tm, f), BF16),
            pltpu.VMEM((out_slots, tm, d // 2), U32),
            pltpu.VMEM((tm, d // 2), U32),
            pltpu.SemaphoreType.DMA((in_slots,)),
            pltpu.SemaphoreType.DMA((out_slots,)),
            pltpu.SemaphoreType.DMA(()),
        ],
    )
    return pl.pallas_call(
        _expert_down_kernel,
        grid_spec=grid_spec,
        out_shape=jax.ShapeDtypeStruct((p, d // 2), U32),
        compiler_params=_cparams(("arbitrary",)),
        name="expert_down",
    )(tile_start, w_down_e, act)


def _shared_dispatch_kernel(pad_ref, pos_ref, h_ref, wg_ref, wu_ref, o_ref, xs_ref, zeros, sem, pad_sem, *, tm):
    i, j = pl.program_id(0), pl.program_id(1)
    nj = pl.num_programs(1)
    step = i * nj + j
    bm, c = h_ref.shape
    share = bm // nj
    kc = c // SHARED_K_PIECES
    per_piece = share // (2 * SHARED_K_PIECES)
    assert per_piece * 2 * SHARED_K_PIECES == share

    @pl.when(step == 0)
    def _():
        _zero_fill_gaps(pad_ref, xs_ref, zeros, pad_sem, tm)

    def scatter_rows(first_row, n_rows, sem_slot):
        for r in range(first_row, first_row + n_rows):
            row = j * share + r
            for k in range(TOP_K):
                p = pos_ref[0, k, row]
                pltpu.make_async_copy(h_ref.at[pl.ds(row, 1)], xs_ref.at[pl.ds(p, 1)],
                                      sem.at[sem_slot]).start(priority=k % DMA_QUEUES)

    def wait_rows(sem_slot):
        for _ in range(TOP_K):
            pltpu.make_async_copy(h_ref.at[pl.ds(0, share)], xs_ref.at[pl.ds(0, share)], sem.at[sem_slot]).wait()

    slot = step % 2
    g = jnp.zeros(o_ref.shape, F32)
    u = jnp.zeros(o_ref.shape, F32)
    done = 0
    for piece in range(SHARED_K_PIECES):
        lo, hi = _unpack_halves(h_ref[:, piece * kc:(piece + 1) * kc])
        for half, xk in ((0, lo.astype(BF16)), (1, hi.astype(BF16))):
            k0 = half * c + piece * kc
            scatter_rows(done, per_piece, slot)
            done += per_piece
            g = g + _dot(xk, wg_ref[k0:k0 + kc, :].astype(BF16))
            u = u + _dot(xk, wu_ref[k0:k0 + kc, :].astype(BF16))
    o_ref[...] = (g * jax.nn.sigmoid(g) * u).astype(o_ref.dtype)

    @pl.when(j > 0)
    def _():
        wait_rows(1 - slot)

    @pl.when(j == nj - 1)
    def _():
        wait_rows(slot)


def _shared_up_dispatch(hp, w_gate_s, w_up_s, pos3, pads, n_rows, tm, bm=1024, bn=256):
    n, c = hp.shape
    f = w_gate_s.shape[1]
    return pl.pallas_call(
        functools.partial(_shared_dispatch_kernel, tm=tm),
        grid=(n // bm, f // bn),
        in_specs=[
            pl.BlockSpec(memory_space=pltpu.SMEM),
            pl.BlockSpec((1, K_PAD, bm), lambda i, j: (i, 0, 0), memory_space=pltpu.SMEM),
            pl.BlockSpec((bm, c), lambda i, j: (i, 0)),
            pl.BlockSpec((2 * c, bn), lambda i, j: (0, j)),
            pl.BlockSpec((2 * c, bn), lambda i, j: (0, j)),
        ],
        out_specs=[pl.BlockSpec((bm, bn), lambda i, j: (i, j)), pl.BlockSpec(memory_space=pl.ANY)],
        out_shape=[jax.ShapeDtypeStruct((n, f), BF16), jax.ShapeDtypeStruct((n_rows, c), U32)],
        scratch_shapes=[pltpu.VMEM((tm // 2, c), U32), pltpu.SemaphoreType.DMA((2,)),
                        pltpu.SemaphoreType.DMA(())],
        compiler_params=_cparams(("arbitrary", "arbitrary")),
        name="shared_up_dispatch",
    )(pads, pos3, hp, w_gate_s, w_up_s)


def _combine_kernel(pos_ref, pos_next_ref, x_ref, s_ref, wt_ref, mod_ref, g_ref, ys_ref, wd_ref, o_ref,
                    ybuf, wd_scr, stage, sem, wd_sem):
    i = pl.program_id(0)
    bt = x_ref.shape[0]
    slot = i % 2
    assert bt % SUBLANES == 0

    @pl.when(i == 0)
    def _():
        rows = stage.shape[0]
        for part in range(wd_scr.shape[0] // rows):
            cp = pltpu.make_async_copy(wd_ref.at[pl.ds(part * rows, rows)], stage, wd_sem)
            cp.start()
            cp.wait()
            wd_scr[part * rows:(part + 1) * rows, :] = stage[...].astype(BF16)

    def row_gather(p_ref, r, dst_slot):
        for k in range(TOP_K):
            p = p_ref[0, k, r]
            pltpu.make_async_copy(ys_ref.at[pl.ds(p, 1)], ybuf.at[dst_slot, k, pl.ds(r, 1)],
                                  sem.at[dst_slot]).start(priority=k % DMA_QUEUES)

    def wait_gather(dst_slot):
        for k in range(TOP_K):
            pltpu.make_async_copy(ys_ref.at[pl.ds(0, bt)], ybuf.at[dst_slot, k], sem.at[dst_slot]).wait()

    @pl.when(i == 0)
    def _():
        def first(r, _):
            row_gather(pos_ref, r, 0)
            return 0

        lax.fori_loop(0, bt, first, 0, unroll=ROW_DMA_UNROLL)

    wait_gather(slot)
    shared = _dot(s_ref[...], wd_scr[...])
    gate = mod_ref[0][5:6]
    c = ybuf.shape[3]
    for j in range(bt // SUBLANES):
        for r in range(j * SUBLANES, (j + 1) * SUBLANES):
            row_gather(pos_next_ref, r, 1 - slot)
        rs = slice(j * SUBLANES, (j + 1) * SUBLANES)
        acc_lo = jnp.zeros((SUBLANES, c), F32)
        acc_hi = jnp.zeros((SUBLANES, c), F32)
        for k in range(TOP_K):
            lo, hi = _unpack_halves(ybuf[slot, k, rs])
            wk = wt_ref[rs, k:k + 1]
            acc_lo = acc_lo + wk * lo
            acc_hi = acc_hi + wk * hi
        moe = jnp.concatenate([acc_lo, acc_hi], axis=1) + shared[rs]
        x = x_ref[rs, :] + gate * moe
        o_ref[rs, :] = x * lax.rsqrt(jnp.mean(x * x, axis=-1, keepdims=True) + EPS) * g_ref[...]

    @pl.when(i == pl.num_programs(0) - 1)
    def _():
        wait_gather(1 - slot)


def _combine(pos3, x1, sact, wts_t, mod, final_g, ys, w_down_s, seq, bt, stage_rows=128):
    n, d = x1.shape
    f = sact.shape[1]
    nb = n // bt
    pos_spec = lambda imap: pl.BlockSpec((1, K_PAD, bt), imap, memory_space=pltpu.SMEM)
    return pl.pallas_call(
        _combine_kernel,
        grid=(nb,),
        in_specs=[
            pos_spec(lambda i: (i, 0, 0)),
            pos_spec(lambda i: (jnp.minimum(i + 1, nb - 1), 0, 0)),
            pl.BlockSpec((bt, d), lambda i: (i, 0)),
            pl.BlockSpec((bt, f), lambda i: (i, 0)),
            pl.BlockSpec((bt, K_PAD), lambda i: (i, 0)),
            pl.BlockSpec((1, N_MOD, d), lambda i: (i * bt // seq, 0, 0)),
            pl.BlockSpec((1, d), lambda i: (0, 0)),
            pl.BlockSpec(memory_space=pl.ANY),
            pl.BlockSpec(memory_space=pl.ANY),
        ],
        out_specs=pl.BlockSpec((bt, d), lambda i: (i, 0)),
        out_shape=jax.ShapeDtypeStruct((n, d), F32),
        scratch_shapes=[
            pltpu.VMEM((2, TOP_K, bt, d // 2), U32),
            pltpu.VMEM((f, d), BF16),
            pltpu.VMEM((stage_rows, d), F32),
            pltpu.SemaphoreType.DMA((2,)),
            pltpu.SemaphoreType.DMA(()),
        ],
        compiler_params=_cparams(("arbitrary",)),
        name="combine",
    )(pos3, pos3, x1, sact, wts_t, mod, final_g.reshape(1, d), ys, w_down_s)


def _moe_tiles(counts, n_tokens, tm):
    e = counts.shape[0]
    n_tiles = n_tokens * TOP_K // tm + e
    tiles_per = (counts + tm - 1) // tm
    tile_end = jnp.cumsum(tiles_per)
    tile_start = jnp.concatenate([jnp.zeros((1,), I32), tile_end.astype(I32)])
    offsets = tile_start[:-1] * tm
    pads = jnp.stack([offsets + counts, tiles_per * tm - counts]).astype(I32)
    return offsets, tile_start, pads, n_tiles


def kernel(x, c, w_ada, b_ada, norm1_g, w_in, w_s, b_s, w_up_a, w_up_b, w_o, norm2_g, w_router,
           router_bias, w_gate_e, w_up_e, w_down_e, w_gate_s, w_up_s, w_down_s, final_g):
    batch, seq, d = x.shape
    n = batch * seq
    depth = w_ada.shape[0]
    a_width = w_up_a.shape[1]
    b_width = w_up_b.shape[1]
    n_heads = b_width // HEAD_DIM
    tm = 256
    bm_dispatch = 1024
    bt_combine = 128

    assert depth == 1, "the combine kernel applies the final norm, so only one layer is supported"
    l = 0
    x2 = x.reshape(n, d)
    mod = _ada(c, w_ada[l], b_ada[l])
    h1 = _norm1(x2, norm1_g[l], mod, seq)
    proj = _inproj(h1, w_in[l], a_width, b_width)
    sgu = _sgu(proj, w_s[l], b_s[l], a_width)
    att = _attention(proj, batch, seq, 2 * a_width, 2 * a_width + b_width,
                     2 * a_width + 2 * b_width, n_heads)
    merged = _merge(sgu, att, w_up_a[l], w_up_b[l], proj,
                    2 * a_width + 3 * b_width, 2 * a_width + 3 * b_width + d)
    x1 = _resid_matmul(merged, w_o[l], x2, mod, 2, seq, "out_proj")

    h2p, ids, wts, ranks, counts = _router(x1, norm2_g[l], mod, w_router[l], router_bias[l], seq)
    offsets, tile_start, pads, n_tiles = _moe_tiles(counts[:, 0], n, tm)
    pos = _positions(offsets, ids, ranks)
    blocked = lambda b: pos.reshape(K_PAD, n // b, b).transpose(1, 0, 2)
    sact, xs = _shared_up_dispatch(h2p, w_gate_s[l], w_up_s[l], blocked(bm_dispatch), pads, n_tiles * tm, tm,
                                   bm=bm_dispatch)
    act = _expert_up(tile_start, xs, w_gate_e[l], w_up_e[l], tm)
    ys = _expert_down(tile_start, act, w_down_e[l], tm)
    out = _combine(blocked(bt_combine), x1, sact, wts.T, mod, final_g, ys, w_down_s[l], seq, bt_combine)
    return out.reshape(batch, seq, d)
```
